```python
import math
import jax, jax.numpy as jnp
from jax import lax
import numpy as np

D_MODEL = 1024
BATCH = 8
SEQ = 2048
DEPTH = 4
DEC_BATCH = 128
DEC_SEQ = 4
PAST_LEN = 2048
PAGE_SIZE = 128

W_GROUP = D_MODEL // 4
D_MIX = 4 * W_GROUP
N_HEADS = 4
HEAD_DIM = W_GROUP // N_HEADS
N_IN_COLS = 9 * W_GROUP
CONV_A = 4
CONV_B = 3
RGLRU_C = 8.0
POOL_WINDOWS = (2, 4, 8, 16)
POOL_BUF = max(POOL_WINDOWS) - 1
POOL_GROUP = W_GROUP // len(POOL_WINDOWS)
D_FF = 4 * D_MODEL
Q_BLOCK = 128
SB_BIAS_INIT = -7.0
ALPHA = (2.0 * DEPTH) ** 0.25
BETA = (8.0 * DEPTH) ** -0.25
LN_EPS = 1e-5

kernel_name = "hybrid_parallel_heads_decode_step"


def layer_norm(x, g, b):
    xf = x.astype(jnp.float32)
    mu = jnp.mean(xf, -1, keepdims=True)
    var = jnp.mean(jnp.square(xf - mu), -1, keepdims=True)
    return ((xf - mu) * lax.rsqrt(var + LN_EPS) * g + b).astype(x.dtype)


def causal_conv(u, buf, w):
    width = w.shape[0]
    T = u.shape[1]
    ext = jnp.concatenate([buf.astype(u.dtype), u], axis=1)
    y = sum(w[j] * ext[:, j:j + T] for j in range(width))
    return y, ext[:, -(width - 1):]


def rglru(xc, h0, wa, ba, wx, bx, lam):
    B, T, _ = xc.shape
    xh = xc.reshape(B, T, N_HEADS, HEAD_DIM)
    r = jax.nn.sigmoid(jnp.einsum('bthi,hij->bthj', xh, wa).reshape(B, T, W_GROUP) + ba)
    i = jax.nn.sigmoid(jnp.einsum('bthi,hij->bthj', xh, wx).reshape(B, T, W_GROUP) + bx)
    log_a = -RGLRU_C * r.astype(jnp.float32) * jax.nn.softplus(-lam.astype(jnp.float32))
    a = jnp.exp(log_a)
    b = jnp.sqrt(-jnp.expm1(2.0 * log_a)) * (i * xc).astype(jnp.float32)

    def combine(c1, c2):
        a1, b1 = c1
        a2, b2 = c2
        return a1 * a2, a2 * b1 + b2

    a_cum, b_cum = lax.associative_scan(combine, (a, b), axis=1)
    h = a_cum * h0.astype(jnp.float32)[:, None, :] + b_cum
    return h.astype(xc.dtype), h[:, -1].astype(xc.dtype)


def pool_mixer(p, buf, p0, w_pool, scale):
    B, T, _ = p.shape
    ext = jnp.concatenate([buf.astype(p.dtype), p], axis=1)
    extf = ext.astype(jnp.float32)
    cs = jnp.concatenate([jnp.zeros_like(extf[:, :1]), jnp.cumsum(extf, axis=1)], axis=1)
    upper = cs[:, POOL_BUF + 1:]
    pos = p0 + jnp.arange(T)
    outs = []
    for g, w in enumerate(POOL_WINDOWS):
        sl = slice(g * POOL_GROUP, (g + 1) * POOL_GROUP)
        lower = cs[:, POOL_BUF + 1 - w:POOL_BUF + 1 - w + T, sl]
        cnt = jnp.minimum(w, pos + 1).astype(jnp.float32)[None, :, None]
        outs.append((upper[..., sl] - lower) / cnt)
    pooled = jnp.concatenate(outs, axis=-1)
    d = (pooled - p.astype(jnp.float32)).astype(p.dtype).reshape(B, T, len(POOL_WINDOWS), POOL_GROUP)
    y = jnp.einsum('btgi,gij->btgj', d, w_pool).reshape(B, T, W_GROUP) * scale
    return y, ext[:, -POOL_BUF:]


def sb_block(q, k, v, q_start, sb_bias):
    Tq, Tk = q.shape[1], k.shape[1]
    z = jnp.einsum('bqhd,bkhd->bhqk', q.astype(jnp.float32), k.astype(jnp.float32)) * (HEAD_DIM ** -0.5)
    z = z + sb_bias.astype(jnp.float32)[None, :, None, None]
    qpos = q_start + jnp.arange(Tq)
    kpos = jnp.arange(Tk)
    mask = kpos[None, :] < qpos[:, None]
    l = jnp.where(mask, jax.nn.log_sigmoid(-z), 0.0)
    later = jnp.flip(jnp.cumsum(jnp.flip(l, -1), -1), -1) - l
    w = jnp.where(mask, jnp.exp(jax.nn.log_sigmoid(z) + later), 0.0)
    return jnp.einsum('bhqk,bkhd->bqhd', w, v.astype(jnp.float32)).astype(q.dtype)


def stick_breaking(q, k_all, v_all, p0, sb_bias):
    T = q.shape[1]
    outs = []
    for qs in range(0, T, Q_BLOCK):
        qe = min(T, qs + Q_BLOCK)
        outs.append(sb_block(q[:, qs:qe], k_all[:, :p0 + qe], v_all[:, :p0 + qe], p0 + qs, sb_bias))
    return jnp.concatenate(outs, axis=1)


def mixer_layer(x, past_k, past_v, h0, buf_a, buf_b, buf_p, p0,
                w_in, conv_a_w, conv_a_b, lru_wa, lru_ba, lru_wx, lru_bx, lru_lambda,
                conv_b_w, sb_bias, pool_w, pool_scale, w_out):
    B, T, _ = x.shape
    proj = x @ w_in
    xa, ga, xb, gb, gc, q, k, v, xp = jnp.split(proj, 9, axis=-1)
    xc, new_buf_a = causal_conv(xa, buf_a, conv_a_w)
    xc = xc + conv_a_b
    ha, new_h = rglru(xc, h0, lru_wa, lru_ba, lru_wx, lru_bx, lru_lambda)
    ya = ha * jax.nn.gelu(ga)
    uc, new_buf_b = causal_conv(gc * xb, buf_b, conv_b_w)
    yb = gb * uc
    qh = q.reshape(B, T, N_HEADS, HEAD_DIM)
    kh = k.reshape(B, T, N_HEADS, HEAD_DIM)
    vh = v.reshape(B, T, N_HEADS, HEAD_DIM)
    if past_k is None:
        k_all, v_all = kh, vh
    else:
        k_all = jnp.concatenate([past_k.astype(kh.dtype), kh], axis=1)
        v_all = jnp.concatenate([past_v.astype(vh.dtype), vh], axis=1)
    yc = stick_breaking(qh, k_all, v_all, p0, sb_bias).reshape(B, T, W_GROUP)
    yd, new_buf_p = pool_mixer(xp, buf_p, p0, pool_w, pool_scale)
    out = jnp.concatenate([ya, yb, yc, yd], axis=-1) @ w_out
    return out, (kh, vh, new_h, new_buf_a, new_buf_b, new_buf_p)


def sq_relu_mlp(x, w1, w2):
    return jnp.square(jax.nn.relu(x @ w1)) @ w2


def trunk(x, p0, cache_k, cache_v, page_table, states, weights):
    (ln_in_g, ln_in_b, w_in, conv_a_w, conv_a_b, lru_wa, lru_ba, lru_wx, lru_bx, lru_lambda,
     conv_b_w, sb_bias, pool_w, pool_scale, w_out, ln1_g, ln1_b, w_ff1, w_ff2, ln2_g, ln2_b) = weights
    h_s, a_s, b_s, p_s = states
    x = layer_norm(x, ln_in_g, ln_in_b)
    new = []
    for l in range(DEPTH):
        if page_table is None:
            pk = pv = None
        else:
            nb, npg = page_table.shape
            pk = cache_k[l][page_table].reshape(nb, npg * PAGE_SIZE, N_HEADS, HEAD_DIM)
            pv = cache_v[l][page_table].reshape(nb, npg * PAGE_SIZE, N_HEADS, HEAD_DIM)
        mix, st = mixer_layer(x, pk, pv, h_s[l], a_s[l], b_s[l], p_s[l], p0,
                              w_in[l], conv_a_w[l], conv_a_b[l], lru_wa[l], lru_ba[l], lru_wx[l],
                              lru_bx[l], lru_lambda[l], conv_b_w[l], sb_bias[l], pool_w[l], pool_scale[l],
                              w_out[l])
        x = layer_norm(ALPHA * x + mix, ln1_g[l], ln1_b[l])
        x = layer_norm(ALPHA * x + sq_relu_mlp(x, w_ff1[l], w_ff2[l]), ln2_g[l], ln2_b[l])
        new.append(st)
    stacked = [jnp.stack([s[j] for s in new]) for j in range(6)]
    return x, stacked


def setup_inputs(seed: int = 0) -> dict:
    key = jax.random.key(seed)
    ks = iter(jax.random.split(key, 48))

    def nrm(shape, s):
        return jax.random.normal(next(ks), shape, jnp.float32) * s

    n_pages = PAST_LEN // PAGE_SIZE
    n_pool = (DEC_BATCH * n_pages * 5) // 4
    page_table = jax.random.permutation(next(ks), n_pool)[:DEC_BATCH * n_pages]
    page_table = page_table.reshape(DEC_BATCH, n_pages).astype(jnp.int32)
    u = jax.random.uniform(next(ks), (DEPTH, W_GROUP), jnp.float32, 0.9, 0.999)
    lru_lambda = jnp.log(u) - jnp.log1p(-u)
    return {
        "x_prompt": nrm((BATCH, SEQ, D_MODEL), 1.0),
        "x_sample": nrm((DEC_BATCH, DEC_SEQ, D_MODEL), 1.0),
        "cache_k": nrm((DEPTH, n_pool, PAGE_SIZE, N_HEADS, HEAD_DIM), 1.0),
        "cache_v": nrm((DEPTH, n_pool, PAGE_SIZE, N_HEADS, HEAD_DIM), 1.0),
        "state_rglru_h": nrm((DEPTH, DEC_BATCH, W_GROUP), 0.5),
        "state_conv_a": nrm((DEPTH, DEC_BATCH, CONV_A - 1, W_GROUP), 1.0),
        "state_conv_b": nrm((DEPTH, DEC_BATCH, CONV_B - 1, W_GROUP), 1.0),
        "state_pool": nrm((DEPTH, DEC_BATCH, POOL_BUF, W_GROUP), 1.0),
        "page_table": page_table,
        "ln_in_g": 1.0 + nrm((D_MODEL,), 0.02),
        "ln_in_b": nrm((D_MODEL,), 0.02),
        "w_in": nrm((DEPTH, D_MODEL, N_IN_COLS), D_MODEL ** -0.5),
        "conv_a_w": nrm((DEPTH, CONV_A, W_GROUP), CONV_A ** -0.5),
        "conv_a_b": nrm((DEPTH, W_GROUP), 0.01),
        "lru_wa": nrm((DEPTH, N_HEADS, HEAD_DIM, HEAD_DIM), HEAD_DIM ** -0.5),
        "lru_ba": nrm((DEPTH, W_GROUP), 0.01),
        "lru_wx": nrm((DEPTH, N_HEADS, HEAD_DIM, HEAD_DIM), HEAD_DIM ** -0.5),
        "lru_bx": nrm((DEPTH, W_GROUP), 0.01),
        "lru_lambda": lru_lambda,
        "conv_b_w": nrm((DEPTH, CONV_B, W_GROUP), CONV_B ** -0.5),
        "sb_bias": SB_BIAS_INIT + nrm((DEPTH, N_HEADS), 0.1),
        "pool_w": nrm((DEPTH, len(POOL_WINDOWS), POOL_GROUP, POOL_GROUP), POOL_GROUP ** -0.5),
        "pool_scale": 1.0 + nrm((DEPTH, W_GROUP), 0.02),
        "w_out": nrm((DEPTH, D_MIX, D_MODEL), BETA * D_MIX ** -0.5),
        "ln1_g": 1.0 + nrm((DEPTH, D_MODEL), 0.02),
        "ln1_b": nrm((DEPTH, D_MODEL), 0.02),
        "w_ff1": nrm((DEPTH, D_MODEL, D_FF), BETA * D_MODEL ** -0.5),
        "w_ff2": nrm((DEPTH, D_FF, D_MODEL), BETA * D_FF ** -0.5),
        "ln2_g": 1.0 + nrm((DEPTH, D_MODEL), 0.02),
        "ln2_b": nrm((DEPTH, D_MODEL), 0.02),
    }


def reference(x_prompt, x_sample, cache_k, cache_v, state_rglru_h, state_conv_a, state_conv_b, state_pool,
              page_table, ln_in_g, ln_in_b, w_in, conv_a_w, conv_a_b, lru_wa, lru_ba, lru_wx, lru_bx,
              lru_lambda, conv_b_w, sb_bias, pool_w, pool_scale, w_out, ln1_g, ln1_b, w_ff1, w_ff2, ln2_g, ln2_b):
    weights = (ln_in_g, ln_in_b, w_in, conv_a_w, conv_a_b, lru_wa, lru_ba, lru_wx, lru_bx, lru_lambda,
               conv_b_w, sb_bias, pool_w, pool_scale, w_out, ln1_g, ln1_b, w_ff1, w_ff2, ln2_g, ln2_b)
    B = x_prompt.shape[0]
    dt = x_prompt.dtype
    prompt_states = (jnp.zeros((DEPTH, B, W_GROUP), dt),
                     jnp.zeros((DEPTH, B, CONV_A - 1, W_GROUP), dt),
                     jnp.zeros((DEPTH, B, CONV_B - 1, W_GROUP), dt),
                     jnp.zeros((DEPTH, B, POOL_BUF, W_GROUP), dt))
    y_prompt, sp = trunk(x_prompt, 0, None, None, None, prompt_states, weights)
    sample_states = (state_rglru_h, state_conv_a, state_conv_b, state_pool)
    past_len = page_table.shape[1] * PAGE_SIZE
    y_sample, ss = trunk(x_sample, past_len, cache_k, cache_v, page_table, sample_states, weights)
    return (y_prompt, y_sample, sp[0], sp[1], ss[0], ss[1], sp[2], ss[2], sp[3], ss[3], sp[4], ss[4], sp[5], ss[5])
```

```python
import functools

import jax
import jax.numpy as jnp
from jax import lax
from jax.experimental import pallas as pl
from jax.experimental.pallas import tpu as pltpu

F32 = jnp.float32
BF16 = jnp.bfloat16

N_HEADS = 4
W_GROUP = 256
HEAD_DIM = W_GROUP // N_HEADS
N_PROJ_GROUPS = 9
CONV_A = 4
CONV_B = 3
RGLRU_C = 8.0
POOL_WINDOWS = (2, 4, 8, 16)
POOL_BUF = max(POOL_WINDOWS) - 1
POOL_GROUP = W_GROUP // len(POOL_WINDOWS)
PAGE_SIZE = 128
LN_EPS = 1e-5
SB_SCALE = HEAD_DIM ** -0.5

VMEM_LIMIT_BYTES = 56 * 1024 * 1024


def _cparams(*sem):
    return pltpu.CompilerParams(dimension_semantics=sem, vmem_limit_bytes=VMEM_LIMIT_BYTES)


def _layer_norm(x, g, b):
    mu = jnp.mean(x, axis=-1, keepdims=True)
    xc = x - mu
    var = jnp.mean(xc * xc, axis=-1, keepdims=True)
    return xc * lax.rsqrt(var + LN_EPS) * g + b


def _softplus(x):
    return jnp.maximum(x, 0.0) + jnp.log1p(jnp.exp(-jnp.abs(x)))


def _dot(a, b):
    return jnp.dot(a, b, preferred_element_type=F32)


def _dot_nt(a, b):
    return lax.dot_general(a, b, (((1,), (1,)), ((), ())), preferred_element_type=F32)


def _suffix_sum(x, tri):
    hi = x.astype(BF16)
    lo = (x - hi.astype(F32)).astype(BF16)
    return _dot(hi, tri) + _dot(lo, tri)


def _ln_kernel(x_ref, g_ref, b_ref, o_ref):
    o_ref[...] = _layer_norm(x_ref[...], g_ref[...], b_ref[...])


def _ln_call(x, g, b, tm):
    n, d = x.shape
    row = pl.BlockSpec((tm, d), lambda i: (i, 0))
    vec = pl.BlockSpec((1, d), lambda i: (0, 0))
    return pl.pallas_call(
        _ln_kernel, grid=(n // tm,), in_specs=[row, vec, vec], out_specs=row,
        out_shape=jax.ShapeDtypeStruct((n, d), F32), compiler_params=_cparams("arbitrary"),
        name="ln_in")(x, g.reshape(1, d), b.reshape(1, d))


def _inproj_kernel(x_ref, w_ref, o_ref):
    o_ref[...] = _dot(x_ref[...].astype(BF16), w_ref[...])


def _inproj_call(x, w, tm):
    n, d = x.shape
    m = w.shape[1]
    return pl.pallas_call(
        _inproj_kernel, grid=(n // tm,),
        in_specs=[pl.BlockSpec((tm, d), lambda i: (i, 0)), pl.BlockSpec((d, m), lambda i: (0, 0))],
        out_specs=pl.BlockSpec((tm, m), lambda i: (i, 0)),
        out_shape=jax.ShapeDtypeStruct((n, m), F32), compiler_params=_cparams("arbitrary"),
        name="in_proj")(x, w)


def _outproj_kernel(yabd_ref, yc_ref, x_ref, wabd_ref, wc_ref, g_ref, b_ref, o_ref, *, alpha):
    mix = _dot(yabd_ref[...].astype(BF16), wabd_ref[...]) + _dot(yc_ref[...].astype(BF16), wc_ref[...])
    o_ref[...] = _layer_norm(alpha * x_ref[...] + mix, g_ref[...], b_ref[...])


def _outproj_call(yabd, yc, x, wabd, wc, g, b, alpha, tm):
    n, d = x.shape
    rows = lambda w: pl.BlockSpec((tm, w), lambda i: (i, 0))
    full = lambda a: pl.BlockSpec(a.shape, lambda i: (0, 0))
    g2, b2 = g.reshape(1, d), b.reshape(1, d)
    return pl.pallas_call(
        functools.partial(_outproj_kernel, alpha=alpha), grid=(n // tm,),
        in_specs=[rows(yabd.shape[1]), rows(yc.shape[1]), rows(d), full(wabd), full(wc), full(g2), full(b2)],
        out_specs=rows(d), out_shape=jax.ShapeDtypeStruct((n, d), F32),
        compiler_params=_cparams("arbitrary"), name="out_proj_ln")(yabd, yc, x, wabd, wc, g2, b2)


def _mlp_kernel(x_ref, w1_ref, w2_ref, g_ref, b_ref, o_ref, acc_ref, *, alpha, fc):
    x = x_ref[...]
    xb = x.astype(BF16)
    d_ff = w1_ref.shape[1]
    for c in range(d_ff // fc):
        h = jnp.maximum(_dot(xb, w1_ref[:, c * fc:(c + 1) * fc]), 0.0)
        part = _dot((h * h).astype(BF16), w2_ref[c * fc:(c + 1) * fc, :])
        if c == 0:
            acc_ref[...] = part
        else:
            acc_ref[...] += part
    o_ref[...] = _layer_norm(alpha * x + acc_ref[...], g_ref[...], b_ref[...])


def _mlp_call(x, w1, w2, g, b, alpha, tm, fc=512):
    n, d = x.shape
    row = pl.BlockSpec((tm, d), lambda i: (i, 0))
    full = lambda a: pl.BlockSpec(a.shape, lambda i: (0, 0))
    g2, b2 = g.reshape(1, d), b.reshape(1, d)
    return pl.pallas_call(
        functools.partial(_mlp_kernel, alpha=alpha, fc=fc), grid=(n // tm,),
        in_specs=[row, full(w1), full(w2), full(g2), full(b2)], out_specs=row,
        out_shape=jax.ShapeDtypeStruct((n, d), F32),
        scratch_shapes=[pltpu.VMEM((tm, d), F32)],
        compiler_params=_cparams("arbitrary"), name="mlp_ln")(x, w1, w2, g2, b2)


def _seqmix_kernel(xa_ref, ga_ref, xb_ref, gb_ref, gc_ref, xp_ref, cnt_ref,
                   h0_ref, bufa_ref, bufb_ref, bufp_ref,
                   caw_ref, cab_ref, wa_ref, ba_ref, wx_ref, bx_ref, lam_ref, cbw_ref, wp_ref, ps_ref,
                   y_ref, ho_ref, ao_ref, bo_ref, po_ref,
                   ea, eb, ep, h_sc, *, tt, nb, cs):
    w = W_GROUP
    i = pl.program_id(0)

    @pl.when(i == 0)
    def _():
        ea[0:CONV_A - 1] = bufa_ref[...]
        eb[0:CONV_B - 1] = bufb_ref[...]
        ep[0:POOL_BUF] = bufp_ref[...]
        h_sc[...] = h0_ref[...]

    @pl.when(i > 0)
    def _():
        ea[0:CONV_A - 1] = ea[tt:tt + CONV_A - 1]
        eb[0:CONV_B - 1] = eb[tt:tt + CONV_B - 1]
        ep[0:POOL_BUF] = ep[tt:tt + POOL_BUF]

    ea[CONV_A - 1:CONV_A - 1 + tt] = xa_ref[...].reshape(tt, nb, w)
    ep[POOL_BUF:POOL_BUF + tt] = xp_ref[...].reshape(tt, nb, w)

    coef = -RGLRU_C * _softplus(-lam_ref[...])
    group = lax.broadcasted_iota(jnp.int32, (1, 1, w), 2) // POOL_GROUP
    rows = cs * nb

    def chunk(k, carry):
        t0 = pl.multiple_of(k * cs, cs)
        r0 = pl.multiple_of(k * rows, rows)

        xc = caw_ref[0:1, :][None] * ea[pl.ds(t0, cs)]
        for j in range(1, CONV_A):
            xc = xc + caw_ref[j:j + 1, :][None] * ea[pl.ds(t0 + j, cs)]
        xc = (xc + cab_ref[...][None]).reshape(rows, w)
        xcb = xc.astype(BF16)
        r = jax.nn.sigmoid(_dot(xcb, wa_ref[...]) + ba_ref[...])
        gi = jax.nn.sigmoid(_dot(xcb, wx_ref[...]) + bx_ref[...])
        log_a = coef * r
        a = jnp.exp(log_a)
        a3 = a.reshape(cs, nb, w)
        b3 = (jnp.sqrt(jnp.tanh(-log_a) * (a * a + 1.0)) * (gi * xc)).reshape(cs, nb, w)
        h = h_sc[...]
        hs = []
        for t in range(cs):
            h = a3[t] * h + b3[t]
            hs.append(h[None])
        h_sc[...] = h
        hseq = jnp.concatenate(hs, axis=0).reshape(rows, w)
        y_ref[pl.ds(r0, rows), 0:w] = hseq * jax.nn.gelu(ga_ref[pl.ds(r0, rows), :])

        u = gc_ref[pl.ds(r0, rows), :] * xb_ref[pl.ds(r0, rows), :]
        eb[pl.ds(t0 + CONV_B - 1, cs)] = u.reshape(cs, nb, w)
        cb = cbw_ref[0:1, :][None] * eb[pl.ds(t0, cs)]
        for j in range(1, CONV_B):
            cb = cb + cbw_ref[j:j + 1, :][None] * eb[pl.ds(t0 + j, cs)]
        y_ref[pl.ds(r0, rows), w:2 * w] = gb_ref[pl.ds(r0, rows), :] * cb.reshape(rows, w)

        e = ep[pl.ds(t0, cs + POOL_BUF)]
        s2 = e[1:] + e[:-1]
        s4 = s2[2:] + s2[:-2]
        s8 = s4[4:] + s4[:-4]
        s16 = s8[8:] + s8[:-8]
        win = jnp.where(group == 0, s2[14:], jnp.where(group == 1, s4[12:], jnp.where(group == 2, s8[8:], s16)))
        d = (win / cnt_ref[pl.ds(t0, cs)] - e[POOL_BUF:]).reshape(rows, w)
        y_ref[pl.ds(r0, rows), 2 * w:3 * w] = _dot(d.astype(BF16), wp_ref[...]) * ps_ref[...]
        return carry

    lax.fori_loop(0, tt // cs, chunk, 0)

    ho_ref[...] = h_sc[...]
    ao_ref[...] = ea[tt:tt + CONV_A - 1]
    bo_ref[...] = eb[tt:tt + CONV_B - 1]
    po_ref[...] = ep[tt:tt + POOL_BUF]


def _seqmix_call(proj, cnt, h0, bufa, bufb, bufp, wts, nb, tt, cs):
    n = proj.shape[0]
    t_total = n // nb
    w = W_GROUP
    assert t_total % tt == 0 and tt % cs == 0 and (tt >= POOL_BUF or tt == t_total)
    col = lambda c: pl.BlockSpec((tt * nb, w), lambda i, c=c: (i, c))
    full = lambda a: pl.BlockSpec(a.shape, lambda i: (0,) * a.ndim)
    state_shapes = [jax.ShapeDtypeStruct(a.shape, F32) for a in (h0, bufa, bufb, bufp)]
    return pl.pallas_call(
        functools.partial(_seqmix_kernel, tt=tt, nb=nb, cs=cs), grid=(t_total // tt,),
        in_specs=[col(0), col(1), col(2), col(3), col(4), col(8),
                  pl.BlockSpec((tt, 1, w), lambda i: (i, 0, 0)),
                  full(h0), full(bufa), full(bufb), full(bufp)] + [full(a) for a in wts],
        out_specs=[pl.BlockSpec((tt * nb, 3 * w), lambda i: (i, 0))] + [full(a) for a in state_shapes],
        out_shape=[jax.ShapeDtypeStruct((n, 3 * w), F32)] + state_shapes,
        scratch_shapes=[pltpu.VMEM((tt + CONV_A - 1, nb, w), F32), pltpu.VMEM((tt + CONV_B - 1, nb, w), F32),
                        pltpu.VMEM((tt + POOL_BUF, nb, w), F32), pltpu.VMEM((nb, w), F32)],
        compiler_params=_cparams("arbitrary"), name="seq_mixers",
    )(proj, proj, proj, proj, proj, proj, cnt, h0, bufa, bufb, bufp, *wts)


def _attn_kernel(bias_ref, q_ref, k_ref, v_ref, tri_ref, o_ref, kb, vb, acc_sc, c_sc, *, bq):
    i = pl.program_id(1)
    hd = HEAD_DIM

    @pl.when(i == 0)
    def _():
        for h in range(N_HEADS):
            kb[h] = k_ref[:, h * hd:(h + 1) * hd].astype(BF16)
            vb[h] = v_ref[:, h * hd:(h + 1) * hd].astype(BF16)

    tri = tri_ref[...]
    qs = [q_ref[:, h * hd:(h + 1) * hd].astype(BF16) for h in range(N_HEADS)]
    causal = lax.broadcasted_iota(jnp.int32, (bq, bq), 1) < lax.broadcasted_iota(jnp.int32, (bq, bq), 0)

    r0 = pl.multiple_of(i * bq, bq)
    for h in range(N_HEADS):
        z = _dot_nt(qs[h], kb[h, pl.ds(r0, bq), :]) * SB_SCALE + bias_ref[h]
        run = _suffix_sum(jnp.where(causal, _softplus(z), 0.0), tri)
        p = jnp.where(causal, jnp.exp(z - run), 0.0)
        acc_sc[h] = _dot(p.astype(BF16), vb[h, pl.ds(r0, bq), :])
        c_sc[h] = run[:, 0:1]

    def body(jj, carry):
        rj = pl.multiple_of((i - 1 - jj) * bq, bq)
        for h in range(N_HEADS):
            z = _dot_nt(qs[h], kb[h, pl.ds(rj, bq), :]) * SB_SCALE + bias_ref[h]
            run = _suffix_sum(_softplus(z), tri) + c_sc[h]
            p = jnp.exp(z - run)
            acc_sc[h] += _dot(p.astype(BF16), vb[h, pl.ds(rj, bq), :])
            c_sc[h] = run[:, 0:1]
        return carry

    lax.fori_loop(0, i, body, 0)
    o_ref[...] = jnp.concatenate([acc_sc[h] for h in range(N_HEADS)], axis=-1)


def _attn_call(qkv, sb_bias, tri, bq):
    nb, t, _ = qkv.shape
    w = W_GROUP
    grid_spec = pltpu.PrefetchScalarGridSpec(
        num_scalar_prefetch=1, grid=(nb, t // bq),
        in_specs=[pl.BlockSpec((None, bq, w), lambda b, i, s: (b, i, 0)),
                  pl.BlockSpec((None, t, w), lambda b, i, s: (b, 0, 1)),
                  pl.BlockSpec((None, t, w), lambda b, i, s: (b, 0, 2)),
                  pl.BlockSpec((bq, bq), lambda b, i, s: (0, 0))],
        out_specs=pl.BlockSpec((None, bq, w), lambda b, i, s: (b, i, 0)),
        scratch_shapes=[pltpu.VMEM((N_HEADS, t, HEAD_DIM), BF16), pltpu.VMEM((N_HEADS, t, HEAD_DIM), BF16),
                        pltpu.VMEM((N_HEADS, bq, HEAD_DIM), F32), pltpu.VMEM((N_HEADS, bq, 1), F32)])
    return pl.pallas_call(
        functools.partial(_attn_kernel, bq=bq), grid_spec=grid_spec,
        out_shape=jax.ShapeDtypeStruct((nb, t, w), F32),
        compiler_params=_cparams("arbitrary", "arbitrary"), name="sb_attention")(sb_bias, qkv, qkv, qkv, tri)


def _paged_attn_kernel(pt_ref, bias_ref, qkv_ref, *refs, tq, npg):
    k_refs, v_refs = refs[:npg], refs[npg:2 * npg]
    tri_ref, o_ref, kt_sc, vt_sc = refs[2 * npg:]
    hd, w, ps = HEAD_DIM, W_GROUP, PAGE_SIZE
    nr = N_HEADS * tq
    for j in range(npg):
        kt_sc[:, j * ps:(j + 1) * ps] = k_refs[j][...].reshape(w, ps).astype(BF16)
        vt_sc[:, j * ps:(j + 1) * ps] = v_refs[j][...].reshape(w, ps).astype(BF16)

    q, kn, vn = qkv_ref[:, 0:w], qkv_ref[:, w:2 * w], qkv_ref[:, 2 * w:3 * w]
    lane_head = lax.broadcasted_iota(jnp.int32, (1, w), 1) // hd
    row = lax.broadcasted_iota(jnp.int32, (nr, 1), 0)
    row_head, row_t = row // tq, row % tq
    qbd = jnp.concatenate([jnp.where(lane_head == h, q, 0.0) for h in range(N_HEADS)], axis=0)
    bias = jnp.zeros((nr, 1), F32)
    for h in range(N_HEADS):
        bias = jnp.where(row_head == h, bias_ref[h], bias)

    run = jnp.zeros((nr, 1), F32)
    acc = jnp.zeros((nr, w), F32)
    for j in range(tq - 1, -1, -1):
        z = jnp.sum(qbd * kn[j:j + 1, :], axis=-1, keepdims=True) * SB_SCALE + bias
        seen = j < row_t
        run = run + jnp.where(seen, _softplus(z), 0.0)
        acc = acc + jnp.where(seen, jnp.exp(z - run), 0.0) * vn[j:j + 1, :]

    z_all = _dot(qbd.astype(BF16), kt_sc[...]) * SB_SCALE + bias
    z2 = jnp.concatenate([z_all[:, j * ps:(j + 1) * ps] for j in range(npg)], axis=0)
    in_page = _suffix_sum(_softplus(z2), tri_ref[...])
    later = [None] * npg
    for j in range(npg - 1, -1, -1):
        later[j] = run
        run = run + in_page[j * nr:(j + 1) * nr, 0:1]
    p2 = jnp.exp(z2 - (in_page + jnp.concatenate(later, axis=0)))
    p_all = jnp.concatenate([p2[j * nr:(j + 1) * nr] for j in range(npg)], axis=1).astype(BF16)
    acc = acc + _dot_nt(p_all, vt_sc[...])
    out = jnp.where(lane_head == 0, acc[0:tq], 0.0)
    for h in range(1, N_HEADS):
        out = jnp.where(lane_head == h, acc[h * tq:(h + 1) * tq], out)
    o_ref[...] = out


def _paged_attn_call(qkv, cache_kt, cache_vt, page_table, sb_bias, tri, layer):
    nb, tq, _ = qkv.shape
    npg = page_table.shape[1]
    w = W_GROUP
    assert (N_HEADS * tq) % 8 == 0
    page = lambda j: pl.BlockSpec((None, None, N_HEADS, HEAD_DIM, PAGE_SIZE),
                                  lambda b, pt, s, j=j: (layer, pt[b, j], 0, 0, 0))
    grid_spec = pltpu.PrefetchScalarGridSpec(
        num_scalar_prefetch=2, grid=(nb,),
        in_specs=[pl.BlockSpec((None, tq, 3 * w), lambda b, pt, s: (b, 0, 0))]
        + [page(j) for j in range(npg)] * 2
        + [pl.BlockSpec((PAGE_SIZE, PAGE_SIZE), lambda b, pt, s: (0, 0))],
        out_specs=pl.BlockSpec((None, tq, w), lambda b, pt, s: (b, 0, 0)),
        scratch_shapes=[pltpu.VMEM((w, npg * PAGE_SIZE), BF16), pltpu.VMEM((w, npg * PAGE_SIZE), BF16)])
    return pl.pallas_call(
        functools.partial(_paged_attn_kernel, tq=tq, npg=npg), grid_spec=grid_spec,
        out_shape=jax.ShapeDtypeStruct((nb, tq, w), F32),
        compiler_params=_cparams("arbitrary"), name="paged_sb_attention",
    )(page_table, sb_bias, qkv, *([cache_kt] * npg), *([cache_vt] * npg), tri)


def _block_diag(wt):
    nh, d, _ = wt.shape
    return jnp.einsum("hij,hg->higj", wt, jnp.eye(nh, dtype=wt.dtype)).reshape(nh * d, nh * d)


def _window_counts(p0, t):
    win = jnp.repeat(jnp.asarray(POOL_WINDOWS, jnp.int32), POOL_GROUP)
    pos = p0 + jnp.arange(t, dtype=jnp.int32)
    return jnp.minimum(win[None, :], pos[:, None] + 1).astype(F32).reshape(t, 1, W_GROUP)


def _to_time_major(a):
    return jnp.swapaxes(a, 0, 1)


def kernel(x_prompt, x_sample, cache_k, cache_v, state_rglru_h, state_conv_a, state_conv_b, state_pool, page_table,
           ln_in_g, ln_in_b, w_in, conv_a_w, conv_a_b, lru_wa, lru_ba, lru_wx, lru_bx, lru_lambda, conv_b_w,
           sb_bias, pool_w, pool_scale, w_out, ln1_g, ln1_b, w_ff1, w_ff2, ln2_g, ln2_b):
    depth = w_in.shape[0]
    alpha = (2.0 * depth) ** 0.25
    bp, tp, d = x_prompt.shape
    bs, ts, _ = x_sample.shape
    past_len = page_table.shape[1] * PAGE_SIZE
    w = W_GROUP

    tm_p = min(512, bp * tp)
    tm_s = min(512, bs * ts)
    bq = min(256, tp)
    tt_p = min(128, tp)
    cs_p = min(16, tt_p)

    xp = _ln_call(_to_time_major(x_prompt).reshape(tp * bp, d), ln_in_g, ln_in_b, tm_p)
    xs = _ln_call(_to_time_major(x_sample).reshape(ts * bs, d), ln_in_g, ln_in_b, tm_s)

    cnt_p = _window_counts(0, tp)
    cnt_s = _window_counts(past_len, ts)
    tri_q = jnp.tril(jnp.ones((bq, bq), BF16))
    tri_page = jnp.tril(jnp.ones((PAGE_SIZE, PAGE_SIZE), BF16))
    cache_kt = jnp.transpose(cache_k, (0, 1, 3, 4, 2))
    cache_vt = jnp.transpose(cache_v, (0, 1, 3, 4, 2))
    zeros =lambda *shape: jnp.zeros(shape, F32)
    row = lambda a: a.reshape(1, -1)

    outs = {name: [] for name in ("kp", "vp", "ks", "vs", "hp", "hs", "ap", "as", "bp", "bs", "pp", "ps")}
    for l in range(depth):
        w_in_l = w_in[l].astype(BF16)
        w_abd = jnp.concatenate([w_out[l, :2 * w], w_out[l, 3 * w:]], axis=0).astype(BF16)
        w_c = w_out[l, 2 * w:3 * w].astype(BF16)
        w1, w2 = w_ff1[l].astype(BF16), w_ff2[l].astype(BF16)
        mix_w = (conv_a_w[l], row(conv_a_b[l]), _block_diag(lru_wa[l]).astype(BF16), row(lru_ba[l]),
                 _block_diag(lru_wx[l]).astype(BF16), row(lru_bx[l]), row(lru_lambda[l]), conv_b_w[l],
                 _block_diag(pool_w[l]).astype(BF16), row(pool_scale[l]))

        proj = _inproj_call(xp, w_in_l, tm_p)
        yabd, h_n, a_n, b_n, p_n = _seqmix_call(
            proj, cnt_p, zeros(bp, w), zeros(CONV_A - 1, bp, w), zeros(CONV_B - 1, bp, w), zeros(POOL_BUF, bp, w),
            mix_w, bp, tt_p, cs_p)
        qkv = _to_time_major(proj[:, 5 * w:8 * w].reshape(tp, bp, 3 * w))
        yc = _attn_call(qkv, sb_bias[l], tri_q, bq)
        yc = _to_time_major(yc).reshape(tp * bp, w)
        xp = _outproj_call(yabd, yc, xp, w_abd, w_c, ln1_g[l], ln1_b[l], alpha, tm_p)
        xp = _mlp_call(xp, w1, w2, ln2_g[l], ln2_b[l], alpha, tm_p)
        outs["kp"].append(qkv[..., w:2 * w].reshape(bp, tp, N_HEADS, HEAD_DIM))
        outs["vp"].append(qkv[..., 2 * w:].reshape(bp, tp, N_HEADS, HEAD_DIM))
        outs["hp"].append(h_n)
        outs["ap"].append(_to_time_major(a_n))
        outs["bp"].append(_to_time_major(b_n))
        outs["pp"].append(_to_time_major(p_n))

        proj = _inproj_call(xs, w_in_l, tm_s)
        yabd, h_n, a_n, b_n, p_n = _seqmix_call(
            proj, cnt_s, state_rglru_h[l], _to_time_major(state_conv_a[l]), _to_time_major(state_conv_b[l]),
            _to_time_major(state_pool[l]), mix_w, bs, ts, ts)
        qkv = _to_time_major(proj[:, 5 * w:8 * w].reshape(ts, bs, 3 * w))
        yc = _paged_attn_call(qkv, cache_kt, cache_vt, page_table, sb_bias[l], tri_page, l)
        yc = _to_time_major(yc).reshape(ts * bs, w)
        xs = _outproj_call(yabd, yc, xs, w_abd, w_c, ln1_g[l], ln1_b[l], alpha, tm_s)
        xs = _mlp_call(xs, w1, w2, ln2_g[l], ln2_b[l], alpha, tm_s)
        outs["ks"].append(qkv[..., w:2 * w].reshape(bs, ts, N_HEADS, HEAD_DIM))
        outs["vs"].append(qkv[..., 2 * w:].reshape(bs, ts, N_HEADS, HEAD_DIM))
        outs["hs"].append(h_n)
        outs["as"].append(_to_time_major(a_n))
        outs["bs"].append(_to_time_major(b_n))
        outs["ps"].append(_to_time_major(p_n))

    y_prompt = _to_time_major(xp.reshape(tp, bp, d))
    y_sample = _to_time_major(xs.reshape(ts, bs, d))
    st = {name: jnp.stack(v) for name, v in outs.items()}
    return (y_prompt, y_sample, st["kp"], st["vp"], st["ks"], st["vs"], st["hp"], st["hs"],
            st["ap"], st["as"], st["bp"], st["bs"], st["pp"], st["ps"])
```

```python
import functools

import jax
import jax.numpy as jnp
from jax import lax
from jax.experimental import pallas as pl
from jax.experimental.pallas import tpu as pltpu

F32 = jnp.float32
BF16 = jnp.bfloat16

N_HEADS = 4
W_GROUP = 256
HEAD_DIM = W_GROUP // N_HEADS
N_PROJ_GROUPS = 9
CONV_A = 4
CONV_B = 3
RGLRU_C = 8.0
POOL_WINDOWS = (2, 4, 8, 16)
POOL_BUF = max(POOL_WINDOWS) - 1
POOL_GROUP = W_GROUP // len(POOL_WINDOWS)
PAGE_SIZE = 128
LN_EPS = 1e-5
SB_SCALE = HEAD_DIM ** -0.5
LANES = 128
LOG2E = 1.4426950408889634

VMEM_LIMIT_BYTES = 56 * 1024 * 1024


def _cparams(*sem):
    return pltpu.CompilerParams(dimension_semantics=sem, vmem_limit_bytes=VMEM_LIMIT_BYTES)


def _layer_norm(x, g, b):
    mu = jnp.mean(x, axis=-1, keepdims=True)
    xc = x - mu
    var = jnp.mean(xc * xc, axis=-1, keepdims=True)
    return xc * lax.rsqrt(var + LN_EPS) * g + b


def _softplus(x):
    return jnp.maximum(x, 0.0) + jnp.log1p(jnp.exp(-jnp.abs(x)))


def _dot(a, b):
    return jnp.dot(a, b, preferred_element_type=F32)


def _dot_nt(a, b):
    return lax.dot_general(a, b, (((1,), (1,)), ((), ())), preferred_element_type=F32)


def _suffix_sum(x, tri):
    hi = x.astype(BF16)
    lo = (x - hi.astype(F32)).astype(BF16)
    return _dot(hi, tri) + _dot(lo, tri)


def _ln_kernel(x_ref, g_ref, b_ref, o_ref):
    o_ref[...] = _layer_norm(x_ref[...], g_ref[...], b_ref[...])


def _ln_call(x, g, b, tm):
    n, d = x.shape
    row = pl.BlockSpec((tm, d), lambda i: (i, 0))
    vec = pl.BlockSpec((1, d), lambda i: (0, 0))
    return pl.pallas_call(
        _ln_kernel, grid=(n // tm,), in_specs=[row, vec, vec], out_specs=row,
        out_shape=jax.ShapeDtypeStruct((n, d), F32), compiler_params=_cparams("arbitrary"),
        name="ln_in")(x, g.reshape(1, d), b.reshape(1, d))


def _inproj_kernel(x_ref, w_ref, o_ref):
    o_ref[...] = _dot(x_ref[...].astype(BF16), w_ref[...])


def _layer_block(a, layer):
    return pl.BlockSpec((None,) + a.shape[1:], lambda i: (layer, 0, 0))


def _inproj_call(x, w, layer, tm):
    n, d = x.shape
    m = w.shape[2]
    return pl.pallas_call(
        _inproj_kernel, grid=(n // tm,),
        in_specs=[pl.BlockSpec((tm, d), lambda i: (i, 0)), _layer_block(w, layer)],
        out_specs=pl.BlockSpec((tm, m), lambda i: (i, 0)),
        out_shape=jax.ShapeDtypeStruct((n, m), F32), compiler_params=_cparams("arbitrary"),
        name="in_proj")(x, w)


def _outproj_kernel(yabd_ref, yc_ref, x_ref, wabd_ref, wc_ref, g_ref, b_ref, o_ref, *, alpha):
    mix = _dot(yabd_ref[...].astype(BF16), wabd_ref[...]) + _dot(yc_ref[...].astype(BF16), wc_ref[...])
    o_ref[...] = _layer_norm(alpha * x_ref[...] + mix, g_ref[...], b_ref[...])


def _outproj_call(yabd, yc, x, wabd, wc, g, b, alpha, tm):
    n, d = x.shape
    rows = lambda w: pl.BlockSpec((tm, w), lambda i: (i, 0))
    full = lambda a: pl.BlockSpec(a.shape, lambda i: (0, 0))
    g2, b2 = g.reshape(1, d), b.reshape(1, d)
    return pl.pallas_call(
        functools.partial(_outproj_kernel, alpha=alpha), grid=(n // tm,),
        in_specs=[rows(yabd.shape[1]), rows(yc.shape[1]), rows(d), full(wabd), full(wc), full(g2), full(b2)],
        out_specs=rows(d), out_shape=jax.ShapeDtypeStruct((n, d), F32),
        compiler_params=_cparams("arbitrary"), name="out_proj_ln")(yabd, yc, x, wabd, wc, g2, b2)


def _mlp_kernel(x_ref, w1_ref, w2_ref, g_ref, b_ref, o_ref, acc_ref, *, alpha, fc):
    x = x_ref[...]
    xb = x.astype(BF16)
    d_ff = w1_ref.shape[1]
    for c in range(d_ff // fc):
        h = jnp.maximum(_dot(xb, w1_ref[:, c * fc:(c + 1) * fc]), 0.0)
        part = _dot((h * h).astype(BF16), w2_ref[c * fc:(c + 1) * fc, :])
        if c == 0:
            acc_ref[...] = part
        else:
            acc_ref[...] += part
    o_ref[...] = _layer_norm(alpha * x + acc_ref[...], g_ref[...], b_ref[...])


def _mlp_call(x, w1, w2, layer, g, b, alpha, tm, fc=512):
    n, d = x.shape
    row = pl.BlockSpec((tm, d), lambda i: (i, 0))
    full = lambda a: pl.BlockSpec(a.shape, lambda i: (0, 0))
    g2, b2 = g.reshape(1, d), b.reshape(1, d)
    return pl.pallas_call(
        functools.partial(_mlp_kernel, alpha=alpha, fc=fc), grid=(n // tm,),
        in_specs=[row, _layer_block(w1, layer), _layer_block(w2, layer), full(g2), full(b2)], out_specs=row,
        out_shape=jax.ShapeDtypeStruct((n, d), F32),
        scratch_shapes=[pltpu.VMEM((tm, d), F32)],
        compiler_params=_cparams("arbitrary"), name="mlp_ln")(x, w1, w2, g2, b2)


def _seqmix_kernel(xa_ref, ga_ref, xb_ref, gb_ref, gc_ref, xp_ref, cnt_ref,
                   h0_ref, bufa_ref, bufb_ref, bufp_ref,
                   caw_ref, cab_ref, wa_ref, ba_ref, wx_ref, bx_ref, lam_ref, cbw_ref, wp_ref, ps_ref,
                   y_ref, ho_ref, ao_ref, bo_ref, po_ref,
                   ea, eb, ep, h_sc, *, tt, nb, cs):
    w = W_GROUP
    i = pl.program_id(0)

    @pl.when(i == 0)
    def _():
        ea[0:CONV_A - 1] = bufa_ref[...]
        eb[0:CONV_B - 1] = bufb_ref[...]
        ep[0:POOL_BUF] = bufp_ref[...]
        h_sc[...] = h0_ref[...]

    @pl.when(i > 0)
    def _():
        ea[0:CONV_A - 1] = ea[tt:tt + CONV_A - 1]
        eb[0:CONV_B - 1] = eb[tt:tt + CONV_B - 1]
        ep[0:POOL_BUF] = ep[tt:tt + POOL_BUF]

    ea[CONV_A - 1:CONV_A - 1 + tt] = xa_ref[...].reshape(tt, nb, w)
    ep[POOL_BUF:POOL_BUF + tt] = xp_ref[...].reshape(tt, nb, w)

    coef = -RGLRU_C * _softplus(-lam_ref[...])
    group = lax.broadcasted_iota(jnp.int32, (1, 1, w), 2) // POOL_GROUP
    rows = cs * nb

    def chunk(k, carry):
        t0 = pl.multiple_of(k * cs, cs)
        r0 = pl.multiple_of(k * rows, rows)

        xc = caw_ref[0:1, :][None] * ea[pl.ds(t0, cs)]
        for j in range(1, CONV_A):
            xc = xc + caw_ref[j:j + 1, :][None] * ea[pl.ds(t0 + j, cs)]
        xc = (xc + cab_ref[...][None]).reshape(rows, w)
        xcb = xc.astype(BF16)
        r = jax.nn.sigmoid(_dot(xcb, wa_ref[...]) + ba_ref[...])
        gi = jax.nn.sigmoid(_dot(xcb, wx_ref[...]) + bx_ref[...])
        log_a = coef * r
        a = jnp.exp(log_a)
        a3 = a.reshape(cs, nb, w)
        b3 = (jnp.sqrt(jnp.tanh(-log_a) * (a * a + 1.0)) * (gi * xc)).reshape(cs, nb, w)
        h = h_sc[...]
        hs = []
        for t in range(cs):
            h = a3[t] * h + b3[t]
            hs.append(h[None])
        h_sc[...] = h
        hseq = jnp.concatenate(hs, axis=0).reshape(rows, w)
        y_ref[pl.ds(r0, rows), 0:w] = hseq * jax.nn.gelu(ga_ref[pl.ds(r0, rows), :])

        u = gc_ref[pl.ds(r0, rows), :] * xb_ref[pl.ds(r0, rows), :]
        eb[pl.ds(t0 + CONV_B - 1, cs)] = u.reshape(cs, nb, w)
        cb = cbw_ref[0:1, :][None] * eb[pl.ds(t0, cs)]
        for j in range(1, CONV_B):
            cb = cb + cbw_ref[j:j + 1, :][None] * eb[pl.ds(t0 + j, cs)]
        y_ref[pl.ds(r0, rows), w:2 * w] = gb_ref[pl.ds(r0, rows), :] * cb.reshape(rows, w)

        e = ep[pl.ds(t0, cs + POOL_BUF)]
        s2 = e[1:] + e[:-1]
        s4 = s2[2:] + s2[:-2]
        s8 = s4[4:] + s4[:-4]
        s16 = s8[8:] + s8[:-8]
        win = jnp.where(group == 0, s2[14:], jnp.where(group == 1, s4[12:], jnp.where(group == 2, s8[8:], s16)))
        d = (win / cnt_ref[pl.ds(t0, cs)] - e[POOL_BUF:]).reshape(rows, w)
        y_ref[pl.ds(r0, rows), 2 * w:3 * w] = _dot(d.astype(BF16), wp_ref[...]) * ps_ref[...]
        return carry

    lax.fori_loop(0, tt // cs, chunk, 0)

    ho_ref[...] = h_sc[...]
    ao_ref[...] = ea[tt:tt + CONV_A - 1]
    bo_ref[...] = eb[tt:tt + CONV_B - 1]
    po_ref[...] = ep[tt:tt + POOL_BUF]


def _seqmix_call(proj, cnt, h0, bufa, bufb, bufp, wts, nb, tt, cs):
    n = proj.shape[0]
    t_total = n // nb
    w = W_GROUP
    assert t_total % tt == 0 and tt % cs == 0 and (tt >= POOL_BUF or tt == t_total)
    col = lambda c: pl.BlockSpec((tt * nb, w), lambda i, c=c: (i, c))
    full = lambda a: pl.BlockSpec(a.shape, lambda i: (0,) * a.ndim)
    state_shapes = [jax.ShapeDtypeStruct(a.shape, F32) for a in (h0, bufa, bufb, bufp)]
    return pl.pallas_call(
        functools.partial(_seqmix_kernel, tt=tt, nb=nb, cs=cs), grid=(t_total // tt,),
        in_specs=[col(0), col(1), col(2), col(3), col(4), col(8),
                  pl.BlockSpec((tt, 1, w), lambda i: (i, 0, 0)),
                  full(h0), full(bufa), full(bufb), full(bufp)] + [full(a) for a in wts],
        out_specs=[pl.BlockSpec((tt * nb, 3 * w), lambda i: (i, 0))] + [full(a) for a in state_shapes],
        out_shape=[jax.ShapeDtypeStruct((n, 3 * w), F32)] + state_shapes,
        scratch_shapes=[pltpu.VMEM((tt + CONV_A - 1, nb, w), F32), pltpu.VMEM((tt + CONV_B - 1, nb, w), F32),
                        pltpu.VMEM((tt + POOL_BUF, nb, w), F32), pltpu.VMEM((nb, w), F32)],
        compiler_params=_cparams("arbitrary"), name="seq_mixers",
    )(proj, proj, proj, proj, proj, proj, cnt, h0, bufa, bufb, bufp, *wts)


def _attn_kernel(bias_ref, q_ref, k_ref, v_ref, tri_ref, o_ref, kb, vb, acc_sc, c_sc, *, bq):
    i = pl.program_id(1)
    hd = HEAD_DIM
    t = k_ref.shape[0]
    heads = range(N_HEADS)
    pad = kb.shape[2] - hd - 2

    @pl.when(i == 0)
    def _():
        for h in heads:
            b = jnp.full((t, 1), bias_ref[h], F32)
            b_hi = b.astype(BF16)
            b_lo = (b - b_hi.astype(F32)).astype(BF16)
            kb[h] = jnp.concatenate(
                [k_ref[:, h * hd:(h + 1) * hd].astype(BF16), b_hi, b_lo, jnp.zeros((t, pad), BF16)], axis=1)
            vb[h] = v_ref[:, h * hd:(h + 1) * hd].astype(BF16)

    tri = tri_ref[...]
    q_tail = jnp.concatenate([jnp.ones((bq, 2), BF16), jnp.zeros((bq, pad), BF16)], axis=1)
    qs = [jnp.concatenate([(q_ref[:, h * hd:(h + 1) * hd] * SB_SCALE).astype(BF16), q_tail], axis=1) for h in heads]
    causal = lax.broadcasted_iota(jnp.int32, (bq, bq), 1) < lax.broadcasted_iota(jnp.int32, (bq, bq), 0)

    def sweep(blocks, first):
        rows = [pl.ds(pl.multiple_of(j * bq, bq), bq) for j, _ in blocks]
        zs = [[_dot_nt(qs[h], kb[h, r, :]) for h in heads] for r in rows]
        lss, sps, col0 = [], [], []
        for (_, mask), zb in zip(blocks, zs):
            lb, sb, cb = [], [], []
            for h in heads:
                z = zb[h]
                sp = jnp.maximum(z, 0.0) + jnp.log(1.0 + jnp.exp2(jnp.abs(z) * -LOG2E))
                lb.append(z - sp)
                if mask is not None:
                    sp = jnp.where(mask, sp, 0.0)
                sb.append(sp.astype(BF16))
                cb.append(sp[:, 0:1])
            lss.append(lb), sps.append(sb), col0.append(cb)
        later = [[_dot(sb[h], tri) for h in heads] for sb in sps]
        carry = [None if first else c_sc[h] for h in heads]
        ps = []
        for n, (_, mask) in enumerate(blocks):
            pb = []
            for h in heads:
                e = lss[n][h] - later[n][h]
                if carry[h] is not None:
                    e = e - carry[h]
                p = jnp.exp(e)
                if mask is not None:
                    p = jnp.where(mask, p, 0.0)
                pb.append(p.astype(BF16))
                total = later[n][h][:, 0:1] + col0[n][h]
                carry[h] = total if carry[h] is None else carry[h] + total
            ps.append(pb)
        for h in heads:
            c_sc[h] = carry[h]
            pv = _dot(ps[0][h], vb[h, rows[0], :])
            for n in range(1, len(blocks)):
                pv = pv + _dot(ps[n][h], vb[h, rows[n], :])
            acc_sc[h] = pv if first else acc_sc[h] + pv

    @pl.when(i % 2 == 0)
    def _():
        sweep([(i, causal)], True)

    @pl.when(i % 2 == 1)
    def _():
        sweep([(i, causal), (i - 1, None)], True)

    top = i - 1 - i % 2

    def body(pair, carry):
        j = top - 2 * pair
        sweep([(j, None), (j - 1, None)], False)
        return carry

    lax.fori_loop(0, i // 2, body, 0)
    o_ref[...] = jnp.concatenate([acc_sc[h] for h in heads], axis=-1)


def _attn_call(qkv, sb_bias, tri, bq):
    nb, t, _ = qkv.shape
    w = W_GROUP
    grid_spec = pltpu.PrefetchScalarGridSpec(
        num_scalar_prefetch=1, grid=(nb, t // bq),
        in_specs=[pl.BlockSpec((None, bq, w), lambda b, i, s: (b, i, 0)),
                  pl.BlockSpec((None, t, w), lambda b, i, s: (b, 0, 1)),
                  pl.BlockSpec((None, t, w), lambda b, i, s: (b, 0, 2)),
                  pl.BlockSpec((bq, bq), lambda b, i, s: (0, 0))],
        out_specs=pl.BlockSpec((None, bq, w), lambda b, i, s: (b, i, 0)),
        scratch_shapes=[pltpu.VMEM((N_HEADS, t, LANES), BF16), pltpu.VMEM((N_HEADS, t, HEAD_DIM), BF16),
                        pltpu.VMEM((N_HEADS, bq, HEAD_DIM), F32), pltpu.VMEM((N_HEADS, bq, 1), F32)])
    return pl.pallas_call(
        functools.partial(_attn_kernel, bq=bq), grid_spec=grid_spec,
        out_shape=jax.ShapeDtypeStruct((nb, t, w), F32),
        compiler_params=_cparams("arbitrary", "arbitrary"), name="sb_attention")(sb_bias, qkv, qkv, qkv, tri)


def _paged_attn_kernel(pt_ref, bias_ref, qkv_ref, *refs, tq, npg):
    k_refs, v_refs = refs[:npg], refs[npg:2 * npg]
    tri_ref, o_ref, kt_sc, vt_sc = refs[2 * npg:]
    hd, w, ps = HEAD_DIM, W_GROUP, PAGE_SIZE
    nr = N_HEADS * tq
    for j in range(npg):
        kt_sc[:, j * ps:(j + 1) * ps] = k_refs[j][...].reshape(w, ps).astype(BF16)
        vt_sc[:, j * ps:(j + 1) * ps] = v_refs[j][...].reshape(w, ps).astype(BF16)

    q, kn, vn = qkv_ref[:, 0:w], qkv_ref[:, w:2 * w], qkv_ref[:, 2 * w:3 * w]
    lane_head = lax.broadcasted_iota(jnp.int32, (1, w), 1) // hd
    row = lax.broadcasted_iota(jnp.int32, (nr, 1), 0)
    row_head, row_t = row // tq, row % tq
    qbd = jnp.concatenate([jnp.where(lane_head == h, q, 0.0) for h in range(N_HEADS)], axis=0)
    bias = jnp.zeros((nr, 1), F32)
    for h in range(N_HEADS):
        bias = jnp.where(row_head == h, bias_ref[h], bias)

    run = jnp.zeros((nr, 1), F32)
    acc = jnp.zeros((nr, w), F32)
    for j in range(tq - 1, -1, -1):
        z = jnp.sum(qbd * kn[j:j + 1, :], axis=-1, keepdims=True) * SB_SCALE + bias
        seen = j < row_t
        run = run + jnp.where(seen, _softplus(z), 0.0)
        acc = acc + jnp.where(seen, jnp.exp(z - run), 0.0) * vn[j:j + 1, :]

    z_all = _dot(qbd.astype(BF16), kt_sc[...]) * SB_SCALE + bias
    z2 = jnp.concatenate([z_all[:, j * ps:(j + 1) * ps] for j in range(npg)], axis=0)
    in_page = _suffix_sum(_softplus(z2), tri_ref[...])
    later = [None] * npg
    for j in range(npg - 1, -1, -1):
        later[j] = run
        run = run + in_page[j * nr:(j + 1) * nr, 0:1]
    p2 = jnp.exp(z2 - (in_page + jnp.concatenate(later, axis=0)))
    p_all = jnp.concatenate([p2[j * nr:(j + 1) * nr] for j in range(npg)], axis=1).astype(BF16)
    acc = acc + _dot_nt(p_all, vt_sc[...])
    out = jnp.where(lane_head == 0, acc[0:tq], 0.0)
    for h in range(1, N_HEADS):
        out = jnp.where(lane_head == h, acc[h * tq:(h + 1) * tq], out)
    o_ref[...] = out


def _paged_attn_call(qkv, cache_kt, cache_vt, page_table, sb_bias, tri, layer):
    nb, tq, _ = qkv.shape
    npg = page_table.shape[1]
    w = W_GROUP
    assert (N_HEADS * tq) % 8 == 0
    page = lambda j: pl.BlockSpec((None, None, N_HEADS, HEAD_DIM, PAGE_SIZE),
                                  lambda b, pt, s, j=j: (layer, pt[b, j], 0, 0, 0))
    grid_spec = pltpu.PrefetchScalarGridSpec(
        num_scalar_prefetch=2, grid=(nb,),
        in_specs=[pl.BlockSpec((None, tq, 3 * w), lambda b, pt, s: (b, 0, 0))]
        + [page(j) for j in range(npg)] * 2
        + [pl.BlockSpec((PAGE_SIZE, PAGE_SIZE), lambda b, pt, s: (0, 0))],
        out_specs=pl.BlockSpec((None, tq, w), lambda b, pt, s: (b, 0, 0)),
        scratch_shapes=[pltpu.VMEM((w, npg * PAGE_SIZE), BF16), pltpu.VMEM((w, npg * PAGE_SIZE), BF16)])
    return pl.pallas_call(
        functools.partial(_paged_attn_kernel, tq=tq, npg=npg), grid_spec=grid_spec,
        out_shape=jax.ShapeDtypeStruct((nb, tq, w), F32),
        compiler_params=_cparams("arbitrary"), name="paged_sb_attention",
    )(page_table, sb_bias, qkv, *([cache_kt] * npg), *([cache_vt] * npg), tri)


def _block_diag(wt):
    nh, d, _ = wt.shape
    return jnp.einsum("hij,hg->higj", wt, jnp.eye(nh, dtype=wt.dtype)).reshape(nh * d, nh * d)


def _window_counts(p0, t):
    win = jnp.repeat(jnp.asarray(POOL_WINDOWS, jnp.int32), POOL_GROUP)
    pos = p0 + jnp.arange(t, dtype=jnp.int32)
    return jnp.minimum(win[None, :], pos[:, None] + 1).astype(F32).reshape(t, 1, W_GROUP)


def _to_time_major(a):
    return jnp.swapaxes(a, 0, 1)


def kernel(x_prompt, x_sample, cache_k, cache_v, state_rglru_h, state_conv_a, state_conv_b, state_pool, page_table,
           ln_in_g, ln_in_b, w_in, conv_a_w, conv_a_b, lru_wa, lru_ba, lru_wx, lru_bx, lru_lambda, conv_b_w,
           sb_bias, pool_w, pool_scale, w_out, ln1_g, ln1_b, w_ff1, w_ff2, ln2_g, ln2_b):
    depth = w_in.shape[0]
    alpha = (2.0 * depth) ** 0.25
    bp, tp, d = x_prompt.shape
    bs, ts, _ = x_sample.shape
    past_len = page_table.shape[1] * PAGE_SIZE
    w = W_GROUP

    tm_p = min(512, bp * tp)
    tm_s = min(512, bs * ts)
    bq = min(256, tp)
    tt_p = min(128, tp)
    cs_p = min(16, tt_p)

    xp = _ln_call(_to_time_major(x_prompt).reshape(tp * bp, d), ln_in_g, ln_in_b, tm_p)
    xs = _ln_call(_to_time_major(x_sample).reshape(ts * bs, d), ln_in_g, ln_in_b, tm_s)

    cnt_p = _window_counts(0, tp)
    cnt_s = _window_counts(past_len, ts)
    tri_q = jnp.tril(jnp.ones((bq, bq), BF16), k=-1)
    tri_page = jnp.tril(jnp.ones((PAGE_SIZE, PAGE_SIZE), BF16))
    cache_kt = jnp.transpose(cache_k, (0, 1, 3, 4, 2))
    cache_vt = jnp.transpose(cache_v, (0, 1, 3, 4, 2))
    zeros = lambda *shape: jnp.zeros(shape, F32)
    row = lambda a: a.reshape(1, -1)

    w_in_b, w1_b, w2_b = w_in.astype(BF16), w_ff1.astype(BF16), w_ff2.astype(BF16)
    outs = {name: [] for name in ("kp", "vp", "ks", "vs", "hp", "hs", "ap", "as", "bp", "bs", "pp", "ps")}
    for l in range(depth):
        w_abd = jnp.concatenate([w_out[l, :2 * w], w_out[l, 3 * w:]], axis=0).astype(BF16)
        w_c = w_out[l, 2 * w:3 * w].astype(BF16)
        mix_w = (conv_a_w[l], row(conv_a_b[l]), _block_diag(lru_wa[l]).astype(BF16), row(lru_ba[l]),
                 _block_diag(lru_wx[l]).astype(BF16), row(lru_bx[l]), row(lru_lambda[l]), conv_b_w[l],
                 _block_diag(pool_w[l]).astype(BF16), row(pool_scale[l]))

        proj = _inproj_call(xp, w_in_b, l, tm_p)
        yabd, h_n, a_n, b_n, p_n = _seqmix_call(
            proj, cnt_p, zeros(bp, w), zeros(CONV_A - 1, bp, w), zeros(CONV_B - 1, bp, w), zeros(POOL_BUF, bp, w),
            mix_w, bp, tt_p, cs_p)
        qkv = _to_time_major(proj[:, 5 * w:8 * w].reshape(tp, bp, 3 * w))
        yc = _attn_call(qkv, sb_bias[l], tri_q, bq)
        yc = _to_time_major(yc).reshape(tp * bp, w)
        xp = _outproj_call(yabd, yc, xp, w_abd, w_c, ln1_g[l], ln1_b[l], alpha, tm_p)
        xp = _mlp_call(xp, w1_b, w2_b, l, ln2_g[l], ln2_b[l], alpha, tm_p)
        outs["kp"].append(qkv[..., w:2 * w].reshape(bp, tp, N_HEADS, HEAD_DIM))
        outs["vp"].append(qkv[..., 2 * w:].reshape(bp, tp, N_HEADS, HEAD_DIM))
        outs["hp"].append(h_n)
        outs["ap"].append(_to_time_major(a_n))
        outs["bp"].append(_to_time_major(b_n))
        outs["pp"].append(_to_time_major(p_n))

        proj = _inproj_call(xs, w_in_b, l, tm_s)
        yabd, h_n, a_n, b_n, p_n = _seqmix_call(
            proj, cnt_s, state_rglru_h[l], _to_time_major(state_conv_a[l]), _to_time_major(state_conv_b[l]),
            _to_time_major(state_pool[l]), mix_w, bs, ts, ts)
        qkv = _to_time_major(proj[:, 5 * w:8 * w].reshape(ts, bs, 3 * w))
        yc = _paged_attn_call(qkv, cache_kt, cache_vt, page_table, sb_bias[l], tri_page, l)
        yc = _to_time_major(yc).reshape(ts * bs, w)
        xs = _outproj_call(yabd, yc, xs, w_abd, w_c, ln1_g[l], ln1_b[l], alpha, tm_s)
        xs = _mlp_call(xs, w1_b, w2_b, l, ln2_g[l], ln2_b[l], alpha, tm_s)
        outs["ks"].append(qkv[..., w:2 * w].reshape(bs, ts, N_HEADS, HEAD_DIM))
        outs["vs"].append(qkv[..., 2 * w:].reshape(bs, ts, N_HEADS, HEAD_DIM))
        outs["hs"].append(h_n)
        outs["as"].append(_to_time_major(a_n))
        outs["bs"].append(_to_time_major(b_n))
        outs["ps"].append(_to_time_major(p_n))

    y_prompt = _to_time_major(xp.reshape(tp, bp, d))
    y_sample = _to_time_major(xs.reshape(ts, bs, d))
    st = {name: jnp.stack(v) for name, v in outs.items()}
    return (y_prompt, y_sample, st["kp"], st["vp"], st["ks"], st["vs"], st["hp"], st["hs"],
            st["ap"], st["as"], st["bp"], st["bs"], st["pp"], st["ps"])
```

```python
import functools

import jax
import jax.numpy as jnp
from jax import lax
from jax.experimental import pallas as pl
from jax.experimental.pallas import tpu as pltpu

F32 = jnp.float32
BF16 = jnp.bfloat16

N_HEADS = 4
W_GROUP = 256
HEAD_DIM = W_GROUP // N_HEADS
N_PROJ_GROUPS = 9
CONV_A = 4
CONV_B = 3
RGLRU_C = 8.0
POOL_WINDOWS = (2, 4, 8, 16)
POOL_BUF = max(POOL_WINDOWS) - 1
POOL_GROUP = W_GROUP // len(POOL_WINDOWS)
PAGE_SIZE = 128
LN_EPS = 1e-5
SB_SCALE = HEAD_DIM ** -0.5
LANES = 128
LOG2E = 1.4426950408889634

VMEM_LIMIT_BYTES = 56 * 1024 * 1024


def _cparams(*sem):
    return pltpu.CompilerParams(dimension_semantics=sem, vmem_limit_bytes=VMEM_LIMIT_BYTES)


def _layer_norm(x, g, b):
    mu = jnp.mean(x, axis=-1, keepdims=True)
    xc = x - mu
    var = jnp.mean(xc * xc, axis=-1, keepdims=True)
    return xc * lax.rsqrt(var + LN_EPS) * g + b


def _softplus(x):
    return jnp.maximum(x, 0.0) + jnp.log1p(jnp.exp(-jnp.abs(x)))


def _dot(a, b):
    return jnp.dot(a, b, preferred_element_type=F32)


def _dot_nt(a, b):
    return lax.dot_general(a, b, (((1,), (1,)), ((), ())), preferred_element_type=F32)


def _suffix_sum(x, tri):
    hi = x.astype(BF16)
    lo = (x - hi.astype(F32)).astype(BF16)
    return _dot(hi, tri) + _dot(lo, tri)


def _rows_from_blocks(src_ref, slab_ref, fn=lambda v: v):
    nb, tt, c = src_ref.shape
    for s in range(nb):
        v = fn(src_ref[s])
        for j in range(c // LANES):
            slab_ref[j, pl.ds(s, tt, stride=nb), :] = v[:, j * LANES:(j + 1) * LANES]
    return jnp.concatenate([slab_ref[j] for j in range(c // LANES)], axis=1)


def _blocks_from_rows(val, slab_ref, dst_ref):
    nb, tt, c = dst_ref.shape
    for j in range(c // LANES):
        slab_ref[j] = val[:, j * LANES:(j + 1) * LANES]
    for s in range(nb):
        for j in range(c // LANES):
            dst_ref[s, :, j * LANES:(j + 1) * LANES] = slab_ref[j, pl.ds(s, tt, stride=nb), :]


def _emit_inproj(xb, w_ref, mix_ref, qkv_ref, slab_ref):
    nmix = mix_ref.shape[-1]
    mix_ref[...] = _dot(xb, w_ref[:, :nmix])
    if slab_ref is None:
        qkv_ref[...] = _dot(xb, w_ref[:, nmix:])
    else:
        _blocks_from_rows(_dot(xb, w_ref[:, nmix:]), slab_ref, qkv_ref)


def _ln_inproj_kernel(x_ref, g_ref, b_ref, w_ref, xo_ref, mix_ref, qkv_ref, *slabs, batch_major):
    norm = lambda v: _layer_norm(v, g_ref[...], b_ref[...])
    x0 = _rows_from_blocks(x_ref, slabs[0], norm) if batch_major else norm(x_ref[...])
    xo_ref[...] = x0
    _emit_inproj(x0.astype(BF16), w_ref, mix_ref, qkv_ref, slabs[1] if batch_major else None)


def _dense_kernel(yabd_ref, yc_ref, x_ref, wabd_ref, wc_ref, g1_ref, b1_ref, w1_ref, w2_ref, g2_ref, b2_ref,
                  *rest, alpha, fc, batch_major, last):
    if last:
        (y_ref, acc_ref), scratch = rest[:2], rest[2:]
    else:
        (win_ref, xo_ref, mix_ref, qkv_ref, acc_ref), scratch = rest[:5], rest[5:]
    yc = _rows_from_blocks(yc_ref, scratch[0]) if batch_major else yc_ref[...]
    mix =_dot(yabd_ref[...].astype(BF16), wabd_ref[...]) + _dot(yc.astype(BF16), wc_ref[...])
    x1 = _layer_norm(alpha * x_ref[...] + mix, g1_ref[...], b1_ref[...])
    xb = x1.astype(BF16)
    for c in range(w1_ref.shape[1] // fc):
        h = jnp.maximum(_dot(xb, w1_ref[:, c * fc:(c + 1) * fc]), 0.0)
        part = _dot((h * h).astype(BF16), w2_ref[c * fc:(c + 1) * fc, :])
        if c == 0:
            acc_ref[...] = part
        else:
            acc_ref[...] += part
    x2 = _layer_norm(alpha * x1 + acc_ref[...], g2_ref[...], b2_ref[...])
    if not last:
        xo_ref[...] = x2
        _emit_inproj(x2.astype(BF16), win_ref, mix_ref, qkv_ref, scratch[1] if batch_major else None)
    elif batch_major:
        _blocks_from_rows(x2, scratch[1], y_ref)
    else:
        y_ref[...] = x2


def _resident(shape, index_map):
    return pl.BlockSpec(shape, index_map, pipeline_mode=pl.Buffered(1))


def _layer_block(a, layer):
    return _resident((None,) + a.shape[1:], lambda i: (layer, 0, 0))


def _slab(rows, c):
    return pltpu.VMEM((c // LANES, rows, LANES), F32)


def _row_specs(nb, t, tm, batch_major):
    rows = lambda c: (pl.BlockSpec((tm, c), lambda i: (i, 0)), (nb * t, c))
    if batch_major:
        blocks = lambda c: (pl.BlockSpec((nb, tm // nb, c), lambda i: (0, i, 0)), (nb, t, c))
    else:
        blocks = rows
    return rows, blocks


def _ln_inproj_call(x, g, b, w_in, nmix, nb, t, tm, batch_major):
    d = x.shape[-1]
    nqkv = w_in.shape[2] - nmix
    rows, blocks = _row_specs(nb, t, tm, batch_major)
    vec = _resident((1, d), lambda i: (0, 0))
    out = [rows(d), rows(nmix), blocks(nqkv)]
    return pl.pallas_call(
        functools.partial(_ln_inproj_kernel, batch_major=batch_major), grid=(nb * t // tm,),
        in_specs=[blocks(d)[0], vec, vec, _layer_block(w_in, 0)],
        out_specs=[s for s, _ in out], out_shape=[jax.ShapeDtypeStruct(shp, F32) for _, shp in out],
        scratch_shapes=[_slab(tm, d), _slab(tm, nqkv)] if batch_major else [],
        compiler_params=_cparams("arbitrary"), name="ln_in_proj")(x, g.reshape(1, d), b.reshape(1, d), w_in)


def _dense_call(yabd, yc, x, wts, layer, nmix, nb, t, tm, alpha, batch_major, fc=512):
    d = x.shape[-1]
    w_in = wts[-1]
    last = layer == w_in.shape[0] - 1
    nqkv = w_in.shape[2] - nmix
    rows, blocks = _row_specs(nb, t, tm, batch_major)
    in_specs = [rows(yabd.shape[-1])[0], blocks(yc.shape[-1])[0], rows(d)[0]] + [_layer_block(a, layer) for a in wts[:-1]]
    scratch = [pltpu.VMEM((tm, d), F32)]
    if batch_major:
        scratch += [_slab(tm, yc.shape[-1]), _slab(tm, d if last else nqkv)]
    if last:
        out = [blocks(d)]
        operands = (yabd, yc, x) + tuple(wts[:-1])
    else:
        in_specs.append(_layer_block(w_in, layer + 1))
        out = [rows(d), rows(nmix), blocks(nqkv)]
        operands = (yabd, yc, x) + tuple(wts)
    res = pl.pallas_call(
        functools.partial(_dense_kernel, alpha=alpha, fc=fc, batch_major=batch_major, last=last),
        grid=(nb * t // tm,), in_specs=in_specs,
        out_specs=[s for s, _ in out], out_shape=[jax.ShapeDtypeStruct(shp, F32) for _, shp in out],
        scratch_shapes=scratch, compiler_params=_cparams("arbitrary"), name="dense_layer")(*operands)
    return res[0] if last else res


def _seqmix_kernel(xa_ref, ga_ref, xb_ref, gb_ref, gc_ref, xp_ref, cnt_ref,
                   h0_ref, bufa_ref, bufb_ref, bufp_ref,
                   caw_ref, cab_ref, wa_ref, ba_ref, wx_ref, bx_ref, lam_ref, cbw_ref, wp_ref, ps_ref,
                   y_ref, ho_ref, ao_ref, bo_ref, po_ref,
                   ea, eb, ep, h_sc, *, tt, nb, cs):
    w = W_GROUP
    i = pl.program_id(0)

    @pl.when(i == 0)
    def _():
        ea[0:CONV_A - 1] = bufa_ref[...]
        eb[0:CONV_B - 1] = bufb_ref[...]
        ep[0:POOL_BUF] = bufp_ref[...]
        h_sc[...] = h0_ref[...]

    @pl.when(i > 0)
    def _():
        ea[0:CONV_A - 1] = ea[tt:tt + CONV_A - 1]
        eb[0:CONV_B - 1] = eb[tt:tt + CONV_B - 1]
        ep[0:POOL_BUF] = ep[tt:tt + POOL_BUF]

    ea[CONV_A - 1:CONV_A - 1 + tt] = xa_ref[...].reshape(tt, nb, w)
    ep[POOL_BUF:POOL_BUF + tt] = xp_ref[...].reshape(tt, nb, w)

    coef = -RGLRU_C * _softplus(-lam_ref[...])
    group = lax.broadcasted_iota(jnp.int32, (1, 1, w), 2) // POOL_GROUP
    rows = cs * nb

    def chunk(k, carry):
        t0 = pl.multiple_of(k * cs, cs)
        r0 = pl.multiple_of(k * rows, rows)

        xc = caw_ref[0:1, :][None] * ea[pl.ds(t0, cs)]
        for j in range(1, CONV_A):
            xc = xc + caw_ref[j:j + 1, :][None] * ea[pl.ds(t0 + j, cs)]
        xc = (xc + cab_ref[...][None]).reshape(rows, w)
        xcb = xc.astype(BF16)
        r = jax.nn.sigmoid(_dot(xcb, wa_ref[...]) + ba_ref[...])
        gi = jax.nn.sigmoid(_dot(xcb, wx_ref[...]) + bx_ref[...])
        log_a = coef * r
        a = jnp.exp(log_a)
        a3 = a.reshape(cs, nb, w)
        b3 = (jnp.sqrt(jnp.tanh(-log_a) * (a * a + 1.0)) * (gi * xc)).reshape(cs, nb, w)
        h = h_sc[...]
        hs = []
        for t in range(cs):
            h = a3[t] * h + b3[t]
            hs.append(h[None])
        h_sc[...] = h
        hseq = jnp.concatenate(hs, axis=0).reshape(rows, w)
        y_ref[pl.ds(r0, rows), 0:w] = hseq * jax.nn.gelu(ga_ref[pl.ds(r0, rows), :])

        u = gc_ref[pl.ds(r0, rows), :] * xb_ref[pl.ds(r0, rows), :]
        eb[pl.ds(t0 + CONV_B - 1, cs)] = u.reshape(cs, nb, w)
        cb = cbw_ref[0:1, :][None] * eb[pl.ds(t0, cs)]
        for j in range(1, CONV_B):
            cb = cb + cbw_ref[j:j + 1, :][None] * eb[pl.ds(t0 + j, cs)]
        y_ref[pl.ds(r0, rows), w:2 * w] = gb_ref[pl.ds(r0, rows), :] * cb.reshape(rows, w)

        e = ep[pl.ds(t0, cs + POOL_BUF)]
        s2 = e[1:] + e[:-1]
        s4 = s2[2:] + s2[:-2]
        s8 = s4[4:] + s4[:-4]
        s16 = s8[8:] + s8[:-8]
        win = jnp.where(group == 0, s2[14:], jnp.where(group == 1, s4[12:], jnp.where(group == 2, s8[8:], s16)))
        d = (win / cnt_ref[pl.ds(t0, cs)] - e[POOL_BUF:]).reshape(rows, w)
        y_ref[pl.ds(r0, rows), 2 * w:3 * w] = _dot(d.astype(BF16), wp_ref[...]) * ps_ref[...]
        return carry

    lax.fori_loop(0, tt // cs, chunk, 0)

    ho_ref[...] = h_sc[...]
    ao_ref[...] = ea[tt:tt + CONV_A - 1]
    bo_ref[...] = eb[tt:tt + CONV_B - 1]
    po_ref[...] = ep[tt:tt + POOL_BUF]


def _seqmix_call(proj, cnt, h0, bufa, bufb, bufp, wts, nb, tt, cs):
    n = proj.shape[0]
    t_total = n // nb
    w = W_GROUP
    assert t_total % tt == 0 and tt % cs == 0 and (tt >= POOL_BUF or tt == t_total)
    col = lambda c: pl.BlockSpec((tt * nb, w), lambda i, c=c: (i, c))
    full = lambda a: pl.BlockSpec(a.shape, lambda i: (0,) * a.ndim)
    state_shapes = [jax.ShapeDtypeStruct(a.shape, F32) for a in (h0, bufa, bufb, bufp)]
    return pl.pallas_call(
        functools.partial(_seqmix_kernel, tt=tt, nb=nb, cs=cs), grid=(t_total // tt,),
        in_specs=[col(0), col(1), col(2), col(3), col(4), col(5),
                  pl.BlockSpec((tt, 1, w), lambda i: (i, 0, 0)),
                  full(h0), full(bufa), full(bufb), full(bufp)] + [full(a) for a in wts],
        out_specs=[pl.BlockSpec((tt * nb, 3 * w), lambda i: (i, 0))] + [full(a) for a in state_shapes],
        out_shape=[jax.ShapeDtypeStruct((n, 3 * w), F32)] + state_shapes,
        scratch_shapes=[pltpu.VMEM((tt + CONV_A - 1, nb, w), F32), pltpu.VMEM((tt + CONV_B - 1, nb, w), F32),
                        pltpu.VMEM((tt + POOL_BUF, nb, w), F32), pltpu.VMEM((nb, w), F32)],
        compiler_params=_cparams("arbitrary"), name="seq_mixers",
    )(proj, proj, proj, proj, proj, proj, cnt, h0, bufa, bufb, bufp, *wts)


def _attn_kernel(bias_ref, q_ref, k_ref, v_ref, tri_ref, o_ref, kb, vb, acc_sc, c_sc, *, bq):
    i = pl.program_id(1)
    hd = HEAD_DIM
    t = k_ref.shape[0]
    heads = range(N_HEADS)
    pad = kb.shape[2] - hd - 2

    @pl.when(i == 0)
    def _():
        for h in heads:
            b = jnp.full((t, 1), bias_ref[h], F32)
            b_hi = b.astype(BF16)
            b_lo = (b - b_hi.astype(F32)).astype(BF16)
            kb[h] = jnp.concatenate(
                [k_ref[:, h * hd:(h + 1) * hd].astype(BF16), b_hi, b_lo, jnp.zeros((t, pad), BF16)], axis=1)
            vb[h] = v_ref[:, h * hd:(h + 1) * hd].astype(BF16)

    tri = tri_ref[...]
    q_tail = jnp.concatenate([jnp.ones((bq, 2), BF16), jnp.zeros((bq, pad), BF16)], axis=1)
    qs = [jnp.concatenate([(q_ref[:, h * hd:(h + 1) * hd] * SB_SCALE).astype(BF16), q_tail], axis=1) for h in heads]
    causal = lax.broadcasted_iota(jnp.int32, (bq, bq), 1) < lax.broadcasted_iota(jnp.int32, (bq, bq), 0)

    def sweep(blocks, first):
        rows = [pl.ds(pl.multiple_of(j * bq, bq), bq) for j, _ in blocks]
        zs = [[_dot_nt(qs[h], kb[h, r, :]) for h in heads] for r in rows]
        lss, sps, col0 = [], [], []
        for (_, mask), zb in zip(blocks, zs):
            lb, sb, cb = [], [], []
            for h in heads:
                z = zb[h]
                sp = jnp.maximum(z, 0.0) + jnp.log(1.0 + jnp.exp2(jnp.abs(z) * -LOG2E))
                lb.append(z - sp)
                if mask is not None:
                    sp = jnp.where(mask, sp, 0.0)
                sb.append(sp.astype(BF16))
                cb.append(sp[:, 0:1])
            lss.append(lb), sps.append(sb), col0.append(cb)
        later = [[_dot(sb[h], tri) for h in heads] for sb in sps]
        carry = [None if first else c_sc[h] for h in heads]
        ps = []
        for n, (_, mask) in enumerate(blocks):
            pb = []
            for h in heads:
                e = lss[n][h] - later[n][h]
                if carry[h] is not None:
                    e = e - carry[h]
                p = jnp.exp(e)
                if mask is not None:
                    p = jnp.where(mask, p, 0.0)
                pb.append(p.astype(BF16))
                total = later[n][h][:, 0:1] + col0[n][h]
                carry[h] = total if carry[h] is None else carry[h] + total
            ps.append(pb)
        for h in heads:
            c_sc[h] = carry[h]
            pv = _dot(ps[0][h], vb[h, rows[0], :])
            for n in range(1, len(blocks)):
                pv = pv + _dot(ps[n][h], vb[h, rows[n], :])
            acc_sc[h] = pv if first else acc_sc[h] + pv

    @pl.when(i % 2 == 0)
    def _():
        sweep([(i, causal)], True)

    @pl.when(i % 2 == 1)
    def _():
        sweep([(i, causal), (i - 1, None)], True)

    top = i - 1 - i % 2

    def body(pair, carry):
        j = top - 2 * pair
        sweep([(j, None), (j - 1, None)], False)
        return carry

    lax.fori_loop(0, i // 2, body, 0)
    o_ref[...] = jnp.concatenate([acc_sc[h] for h in heads], axis=-1)


def _attn_call(qkv, sb_bias, tri, bq):
    nb, t, _ = qkv.shape
    w = W_GROUP
    grid_spec = pltpu.PrefetchScalarGridSpec(
        num_scalar_prefetch=1, grid=(nb, t // bq),
        in_specs=[pl.BlockSpec((None, bq, w), lambda b, i, s: (b, i, 0)),
                  pl.BlockSpec((None, t, w), lambda b, i, s: (b, 0, 1)),
                  pl.BlockSpec((None, t, w), lambda b, i, s: (b, 0, 2)),
                  pl.BlockSpec((bq, bq), lambda b, i, s: (0, 0))],
        out_specs=pl.BlockSpec((None, bq, w), lambda b, i, s: (b, i, 0)),
        scratch_shapes=[pltpu.VMEM((N_HEADS, t, LANES), BF16), pltpu.VMEM((N_HEADS, t, HEAD_DIM), BF16),
                        pltpu.VMEM((N_HEADS, bq, HEAD_DIM), F32), pltpu.VMEM((N_HEADS, bq, 1), F32)])
    return pl.pallas_call(
        functools.partial(_attn_kernel, bq=bq), grid_spec=grid_spec,
        out_shape=jax.ShapeDtypeStruct((nb, t, w), F32),
        compiler_params=_cparams("arbitrary", "arbitrary"), name="sb_attention")(sb_bias, qkv, qkv, qkv, tri)


def _paged_attn_kernel(pt_ref, bias_ref, qkv_ref, *refs, tq, npg):
    k_refs, v_refs = refs[:npg], refs[npg:2 * npg]
    tri_ref, o_ref, kt_sc, vt_sc = refs[2 * npg:]
    hd, w, ps = HEAD_DIM, W_GROUP, PAGE_SIZE
    nr = N_HEADS * tq
    for j in range(npg):
        kt_sc[:, j * ps:(j + 1) * ps] = k_refs[j][...].reshape(w, ps).astype(BF16)
        vt_sc[:, j * ps:(j + 1) * ps] = v_refs[j][...].reshape(w, ps).astype(BF16)

    q, kn, vn = qkv_ref[:, 0:w], qkv_ref[:, w:2 * w], qkv_ref[:, 2 * w:3 * w]
    lane_head = lax.broadcasted_iota(jnp.int32, (1, w), 1) // hd
    row = lax.broadcasted_iota(jnp.int32, (nr, 1), 0)
    row_head, row_t = row // tq, row % tq
    qbd = jnp.concatenate([jnp.where(lane_head == h, q, 0.0) for h in range(N_HEADS)], axis=0)
    bias = jnp.zeros((nr, 1), F32)
    for h in range(N_HEADS):
        bias = jnp.where(row_head == h, bias_ref[h], bias)

    run = jnp.zeros((nr, 1), F32)
    acc = jnp.zeros((nr, w), F32)
    for j in range(tq - 1, -1, -1):
        z = jnp.sum(qbd * kn[j:j + 1, :], axis=-1, keepdims=True) * SB_SCALE + bias
        seen = j < row_t
        run = run + jnp.where(seen, _softplus(z), 0.0)
        acc = acc + jnp.where(seen, jnp.exp(z - run), 0.0) * vn[j:j + 1, :]

    z_all = _dot(qbd.astype(BF16), kt_sc[...]) * SB_SCALE + bias
    z2 = jnp.concatenate([z_all[:, j * ps:(j + 1) * ps] for j in range(npg)], axis=0)
    in_page = _suffix_sum(_softplus(z2), tri_ref[...])
    later = [None] * npg
    for j in range(npg - 1, -1, -1):
        later[j] = run
        run = run + in_page[j * nr:(j + 1) * nr, 0:1]
    p2 = jnp.exp(z2 - (in_page + jnp.concatenate(later, axis=0)))
    p_all = jnp.concatenate([p2[j * nr:(j + 1) * nr] for j in range(npg)], axis=1).astype(BF16)
    acc = acc + _dot_nt(p_all, vt_sc[...])
    out = jnp.where(lane_head == 0, acc[0:tq], 0.0)
    for h in range(1, N_HEADS):
        out = jnp.where(lane_head == h, acc[h * tq:(h + 1) * tq], out)
    o_ref[...] = out


def _paged_attn_call(qkv, cache_kt, cache_vt, page_table, sb_bias, tri, layer):
    nb, tq, _ = qkv.shape
    npg = page_table.shape[1]
    w = W_GROUP
    assert (N_HEADS * tq) % 8 == 0
    page = lambda j: pl.BlockSpec((None, None, N_HEADS, HEAD_DIM, PAGE_SIZE),
                                  lambda b, pt, s, j=j: (layer, pt[b, j], 0, 0, 0))
    grid_spec = pltpu.PrefetchScalarGridSpec(
        num_scalar_prefetch=2, grid=(nb,),
        in_specs=[pl.BlockSpec((None, tq, 3 * w), lambda b, pt, s: (b, 0, 0))]
        + [page(j) for j in range(npg)] * 2
        + [pl.BlockSpec((PAGE_SIZE, PAGE_SIZE), lambda b, pt, s: (0, 0))],
        out_specs=pl.BlockSpec((None, tq, w), lambda b, pt, s: (b, 0, 0)),
        scratch_shapes=[pltpu.VMEM((w, npg * PAGE_SIZE), BF16), pltpu.VMEM((w, npg * PAGE_SIZE), BF16)])
    return pl.pallas_call(
        functools.partial(_paged_attn_kernel, tq=tq, npg=npg), grid_spec=grid_spec,
        out_shape=jax.ShapeDtypeStruct((nb, tq, w), F32),
        compiler_params=_cparams("arbitrary"), name="paged_sb_attention",
    )(page_table, sb_bias, qkv, *([cache_kt] * npg), *([cache_vt] * npg), tri)


def _block_diag(wt):
    nh, d, _ = wt.shape
    return jnp.einsum("hij,hg->higj", wt, jnp.eye(nh, dtype=wt.dtype)).reshape(nh * d, nh * d)


def _window_counts(p0, t):
    win = jnp.repeat(jnp.asarray(POOL_WINDOWS, jnp.int32), POOL_GROUP)
    pos = p0 + jnp.arange(t, dtype=jnp.int32)
    return jnp.minimum(win[None, :], pos[:, None] + 1).astype(F32).reshape(t, 1, W_GROUP)


def _to_time_major(a):
    return jnp.swapaxes(a, 0, 1)


def kernel(x_prompt, x_sample, cache_k, cache_v, state_rglru_h, state_conv_a, state_conv_b, state_pool, page_table,
           ln_in_g, ln_in_b, w_in, conv_a_w, conv_a_b, lru_wa, lru_ba, lru_wx, lru_bx, lru_lambda, conv_b_w,
           sb_bias, pool_w, pool_scale, w_out, ln1_g, ln1_b, w_ff1, w_ff2, ln2_g, ln2_b):
    depth = w_in.shape[0]
    alpha = (2.0 * depth) ** 0.25
    bp, tp, d = x_prompt.shape
    bs, ts, _ = x_sample.shape
    past_len = page_table.shape[1] * PAGE_SIZE
    w = W_GROUP

    tm_p = min(512, bp * tp)
    tm_s = min(512, bs * ts)
    bq = min(256, tp)
    tt_p = min(128, tp)
    cs_p = min(16, tt_p)

    cnt_p = _window_counts(0, tp)
    cnt_s = _window_counts(past_len, ts)
    tri_q = jnp.tril(jnp.ones((bq, bq), BF16), k=-1)
    tri_page = jnp.tril(jnp.ones((PAGE_SIZE, PAGE_SIZE), BF16))
    cache_kt = jnp.transpose(cache_k, (0, 1, 3, 4, 2))
    cache_vt = jnp.transpose(cache_v, (0, 1, 3, 4, 2))
    zeros = lambda *shape: jnp.zeros(shape, F32)
    row = lambda a: a.reshape(1, -1)

    nmix = (N_PROJ_GROUPS - 3) * w
    w_in_b = jnp.concatenate([w_in[..., :5 * w], w_in[..., 8 * w:], w_in[..., 5 * w:8 * w]], axis=-1).astype(BF16)
    stack_row = lambda a: a.reshape(depth, 1, -1)
    dense_w = (jnp.concatenate([w_out[:, :2 * w], w_out[:, 3 * w:]], axis=1).astype(BF16),
               w_out[:, 2 * w:3 * w].astype(BF16), stack_row(ln1_g), stack_row(ln1_b),
               w_ff1.astype(BF16), w_ff2.astype(BF16), stack_row(ln2_g), stack_row(ln2_b), w_in_b)

    xp, mix_p, qkv_p = _ln_inproj_call(x_prompt, ln_in_g, ln_in_b, w_in_b, nmix, bp, tp, tm_p, True)
    xs, mix_s, qkv_s = _ln_inproj_call(
        _to_time_major(x_sample).reshape(ts * bs, d), ln_in_g, ln_in_b, w_in_b, nmix, bs, ts, tm_s, False)

    outs = {name: [] for name in ("kp", "vp", "ks", "vs", "hp", "hs", "ap", "as", "bp", "bs", "pp", "ps")}
    for l in range(depth):
        mix_w = (conv_a_w[l], row(conv_a_b[l]), _block_diag(lru_wa[l]).astype(BF16), row(lru_ba[l]),
                 _block_diag(lru_wx[l]).astype(BF16), row(lru_bx[l]), row(lru_lambda[l]), conv_b_w[l],
                 _block_diag(pool_w[l]).astype(BF16), row(pool_scale[l]))

        yabd, h_n, a_n, b_n, p_n = _seqmix_call(
            mix_p, cnt_p, zeros(bp, w), zeros(CONV_A - 1, bp, w), zeros(CONV_B - 1, bp, w), zeros(POOL_BUF, bp, w),
            mix_w, bp, tt_p, cs_p)
        yc = _attn_call(qkv_p, sb_bias[l], tri_q, bq)
        outs["kp"].append(qkv_p[..., w:2 * w].reshape(bp, tp, N_HEADS, HEAD_DIM))
        outs["vp"].append(qkv_p[..., 2 * w:].reshape(bp, tp, N_HEADS, HEAD_DIM))
        res = _dense_call(yabd, yc, xp, dense_w, l, nmix, bp, tp, tm_p, alpha, True)
        if l == depth - 1:
            y_prompt = res
        else:
            xp, mix_p, qkv_p = res
        outs["hp"].append(h_n)
        outs["ap"].append(_to_time_major(a_n))
        outs["bp"].append(_to_time_major(b_n))
        outs["pp"].append(_to_time_major(p_n))

        yabd, h_n, a_n, b_n, p_n = _seqmix_call(
            mix_s, cnt_s, state_rglru_h[l], _to_time_major(state_conv_a[l]), _to_time_major(state_conv_b[l]),
            _to_time_major(state_pool[l]), mix_w, bs, ts, ts)
        qkv = _to_time_major(qkv_s.reshape(ts, bs, 3 * w))
        yc = _paged_attn_call(qkv, cache_kt, cache_vt, page_table, sb_bias[l], tri_page, l)
        yc = _to_time_major(yc).reshape(ts * bs, w)
        res = _dense_call(yabd, yc, xs, dense_w, l, nmix, bs, ts, tm_s, alpha, False)
        if l == depth - 1:
            y_sample = _to_time_major(res.reshape(ts, bs, d))
        else:
            xs, mix_s, qkv_s = res
        outs["ks"].append(qkv[..., w:2 * w].reshape(bs, ts, N_HEADS, HEAD_DIM))
        outs["vs"].append(qkv[..., 2 * w:].reshape(bs, ts, N_HEADS, HEAD_DIM))
        outs["hs"].append(h_n)
        outs["as"].append(_to_time_major(a_n))
        outs["bs"].append(_to_time_major(b_n))
        outs["ps"].append(_to_time_major(p_n))

    st = {name: jnp.stack(v) for name, v in outs.items()}
    return (y_prompt, y_sample, st["kp"], st["vp"], st["ks"], st["vs"], st["hp"], st["hs"],
            st["ap"], st["as"], st["bp"], st["bs"], st["pp"], st["ps"])
```

```python
import functools

import jax
import jax.numpy as jnp
from jax import lax
from jax.experimental import pallas as pl
from jax.experimental.pallas import tpu as pltpu

F32 = jnp.float32
BF16 = jnp.bfloat16

N_HEADS = 4
W_GROUP = 256
HEAD_DIM = W_GROUP // N_HEADS
N_PROJ_GROUPS = 9
CONV_A = 4
CONV_B = 3
RGLRU_C = 8.0
POOL_WINDOWS = (2, 4, 8, 16)
POOL_BUF = max(POOL_WINDOWS) - 1
POOL_GROUP = W_GROUP // len(POOL_WINDOWS)
PAGE_SIZE = 128
LN_EPS = 1e-5
SB_SCALE = HEAD_DIM ** -0.5
LANES = 128
LOG2E = 1.4426950408889634

VMEM_LIMIT_BYTES = 56 * 1024 * 1024


def _cparams(*sem):
    return pltpu.CompilerParams(dimension_semantics=sem, vmem_limit_bytes=VMEM_LIMIT_BYTES)


def _layer_norm(x, g, b):
    mu = jnp.mean(x, axis=-1, keepdims=True)
    xc = x - mu
    var = jnp.mean(xc * xc, axis=-1, keepdims=True)
    return xc * lax.rsqrt(var + LN_EPS) * g + b


def _softplus(x):
    return jnp.maximum(x, 0.0) + jnp.log1p(jnp.exp(-jnp.abs(x)))


def _dot(a, b):
    return jnp.dot(a, b, preferred_element_type=F32)


def _dot_nt(a, b):
    return lax.dot_general(a, b, (((1,), (1,)), ((), ())), preferred_element_type=F32)


def _suffix_sum(x, tri):
    hi = x.astype(BF16)
    lo = (x - hi.astype(F32)).astype(BF16)
    return _dot(hi, tri) + _dot(lo, tri)


def _rows_from_blocks(src_ref, slab_ref, fn=lambda v: v):
    nb, tt, c = src_ref.shape
    for s in range(nb):
        v = fn(src_ref[s])
        for j in range(c // LANES):
            slab_ref[j, pl.ds(s, tt, stride=nb), :] = v[:, j * LANES:(j + 1) * LANES]
    return jnp.concatenate([slab_ref[j] for j in range(c // LANES)], axis=1)


def _blocks_from_rows(val, slab_ref, dst_ref, copies=()):
    nb, tt, c = dst_ref.shape
    for j in range(c // LANES):
        slab_ref[j] = val[:, j * LANES:(j + 1) * LANES]
    for s in range(nb):
        for j in range(c // LANES):
            piece = slab_ref[j, pl.ds(s, tt, stride=nb), :]
            dst_ref[s, :, j * LANES:(j + 1) * LANES] = piece
            for ref, c0 in copies:
                if c0 <= j * LANES < c0 + ref.shape[-1]:
                    ref[s, :, j * LANES - c0:(j + 1) * LANES - c0] = piece


def _emit_inproj(xb, w_ref, mix_ref, qkv_ref, kv_refs, slab_ref):
    nmix = mix_ref.shape[-1]
    mix_ref[...] = _dot(xb, w_ref[:, :nmix])
    if slab_ref is None:
        qkv_ref[...] = _dot(xb, w_ref[:, nmix:])
    else:
        k_ref, v_ref = kv_refs
        _blocks_from_rows(_dot(xb, w_ref[:, nmix:]), slab_ref, qkv_ref, ((k_ref, W_GROUP), (v_ref, 2 * W_GROUP)))


def _ln_inproj_kernel(x_ref, g_ref, b_ref, w_ref, *rest, batch_major):
    norm = lambda v: _layer_norm(v, g_ref[...], b_ref[...])
    if batch_major:
        xo_ref, mix_ref, qkv_ref, k_ref, v_ref, slab_x, slab_qkv = rest[2:]
        x0 = _rows_from_blocks(x_ref, slab_x, norm)
        xo_ref[...] = x0
        _emit_inproj(x0.astype(BF16), w_ref, mix_ref, qkv_ref, (k_ref, v_ref), slab_qkv)
    else:
        xo_ref, mix_ref, qkv_ref = rest
        x0 = norm(x_ref[...])
        xo_ref[...] = x0
        _emit_inproj(x0.astype(BF16), w_ref, mix_ref, qkv_ref, None, None)


def _dense_kernel(yabd_ref, yc_ref, x_ref, wabd_ref, wc_ref, g1_ref, b1_ref, w1_ref, w2_ref, g2_ref, b2_ref,
                  *rest, alpha, fc, batch_major, last):
    kv_refs = None
    if last:
        (y_ref, acc_ref), scratch = rest[:2], rest[2:]
    elif batch_major:
        win_ref, xo_ref, mix_ref, qkv_ref, k_ref, v_ref, acc_ref = rest[:1] + rest[3:9]
        kv_refs, scratch = (k_ref, v_ref), rest[9:]
    else:
        (win_ref, xo_ref, mix_ref, qkv_ref, acc_ref), scratch = rest[:5], rest[5:]
    yc = _rows_from_blocks(yc_ref, scratch[0]) if batch_major else yc_ref[...]
    mix = _dot(yabd_ref[...].astype(BF16), wabd_ref[...]) + _dot(yc.astype(BF16), wc_ref[...])
    x1 = _layer_norm(alpha * x_ref[...] + mix, g1_ref[...], b1_ref[...])
    xb = x1.astype(BF16)
    for c in range(w1_ref.shape[1] // fc):
        h = jnp.maximum(_dot(xb, w1_ref[:, c * fc:(c + 1) * fc]), 0.0)
        part = _dot((h * h).astype(BF16), w2_ref[c * fc:(c + 1) * fc, :])
        if c == 0:
            acc_ref[...] = part
        else:
            acc_ref[...] += part
    x2 = _layer_norm(alpha * x1 + acc_ref[...], g2_ref[...], b2_ref[...])
    if not last:
        xo_ref[...] = x2
        _emit_inproj(x2.astype(BF16), win_ref, mix_ref, qkv_ref, kv_refs, scratch[1] if batch_major else None)
    elif batch_major:
        _blocks_from_rows(x2, scratch[1], y_ref)
    else:
        y_ref[...] = x2


def _resident(shape, index_map):
    return pl.BlockSpec(shape, index_map, pipeline_mode=pl.Buffered(1))


def _layer_block(a, layer):
    return _resident((None,) + a.shape[1:], lambda i: (layer, 0, 0))


def _slab(rows, c):
    return pltpu.VMEM((c // LANES, rows, LANES), F32)


def _row_specs(nb, t, tm, batch_major):
    rows = lambda c: (pl.BlockSpec((tm, c), lambda i: (i, 0)), (nb * t, c))
    if batch_major:
        blocks = lambda c: (pl.BlockSpec((nb, tm // nb, c), lambda i: (0, i, 0)), (nb, t, c))
    else:
        blocks = rows
    return rows, blocks


def _kv_stack_spec(depth, nb, t, tm, layer):
    return (pl.BlockSpec((None, nb, tm // nb, W_GROUP), lambda i: (layer, 0, i, 0)), (depth, nb, t, W_GROUP))


def _ln_inproj_call(x, g, b, w_in, nmix, nb, t, tm, batch_major):
    d = x.shape[-1]
    depth = w_in.shape[0]
    nqkv = w_in.shape[2] - nmix
    rows, blocks = _row_specs(nb, t, tm, batch_major)
    vec = _resident((1, d), lambda i: (0, 0))
    in_specs = [blocks(d)[0], vec, vec, _layer_block(w_in, 0)]
    operands = (x, g.reshape(1, d), b.reshape(1, d), w_in)
    out = [rows(d), rows(nmix), blocks(nqkv)]
    aliases = {}
    if batch_major:
        aliases = {len(operands): 3, len(operands) + 1: 4}
        in_specs += [pl.BlockSpec(memory_space=pl.ANY)] * 2
        operands += (jnp.zeros((depth, nb, t, W_GROUP), F32), jnp.zeros((depth, nb, t, W_GROUP), F32))
        out += [_kv_stack_spec(depth, nb, t, tm, 0)] * 2
    return pl.pallas_call(
        functools.partial(_ln_inproj_kernel, batch_major=batch_major), grid=(nb * t // tm,),
        in_specs=in_specs, out_specs=[s for s, _ in out],
        out_shape=[jax.ShapeDtypeStruct(shp, F32) for _, shp in out],
        scratch_shapes=[_slab(tm, d), _slab(tm, nqkv)] if batch_major else [],
        input_output_aliases=aliases, compiler_params=_cparams("arbitrary"), name="ln_in_proj")(*operands)


def _dense_call(yabd, yc, x, wts, kv_stacks, layer, nmix, nb, t, tm, alpha, batch_major, fc=512):
    d = x.shape[-1]
    w_in = wts[-1]
    depth = w_in.shape[0]
    last = layer == depth - 1
    nqkv = w_in.shape[2] - nmix
    rows, blocks = _row_specs(nb, t, tm, batch_major)
    in_specs = [rows(yabd.shape[-1])[0], blocks(yc.shape[-1])[0], rows(d)[0]] + [_layer_block(a, layer) for a in wts[:-1]]
    scratch = [pltpu.VMEM((tm, d), F32)]
    aliases = {}
    if batch_major:
        scratch += [_slab(tm, yc.shape[-1]), _slab(tm, d if last else nqkv)]
    if last:
        out = [blocks(d)]
        operands = (yabd, yc, x) + tuple(wts[:-1])
    else:
        in_specs.append(_layer_block(w_in, layer + 1))
        out = [rows(d), rows(nmix), blocks(nqkv)]
        operands = (yabd, yc, x) + tuple(wts)
        if batch_major:
            aliases = {len(operands): 3, len(operands) + 1: 4}
            in_specs += [pl.BlockSpec(memory_space=pl.ANY)] * 2
            out += [_kv_stack_spec(depth, nb, t, tm, layer + 1)] * 2
            operands += tuple(kv_stacks)
    res = pl.pallas_call(
        functools.partial(_dense_kernel, alpha=alpha, fc=fc, batch_major=batch_major, last=last),
        grid=(nb * t // tm,), in_specs=in_specs,
        out_specs=[s for s, _ in out], out_shape=[jax.ShapeDtypeStruct(shp, F32) for _, shp in out],
        scratch_shapes=scratch, input_output_aliases=aliases,
        compiler_params=_cparams("arbitrary"), name="dense_layer")(*operands)
    return res[0] if last else res


def _seqmix_kernel(xa_ref, ga_ref, xb_ref, gb_ref, gc_ref, xp_ref, cnt_ref,
                   h0_ref, bufa_ref, bufb_ref, bufp_ref,
                   caw_ref, cab_ref, wa_ref, ba_ref, wx_ref, bx_ref, lam_ref, cbw_ref, wp_ref, ps_ref,
                   y_ref, ho_ref, ao_ref, bo_ref, po_ref,
                   ea, eb, ep, h_sc, *, tt, nb, cs):
    w = W_GROUP
    i = pl.program_id(0)

    @pl.when(i == 0)
    def _():
        ea[0:CONV_A - 1] = bufa_ref[...]
        eb[0:CONV_B - 1] = bufb_ref[...]
        ep[0:POOL_BUF] = bufp_ref[...]
        h_sc[...] = h0_ref[...]

    @pl.when(i > 0)
    def _():
        ea[0:CONV_A - 1] = ea[tt:tt + CONV_A - 1]
        eb[0:CONV_B - 1] = eb[tt:tt + CONV_B - 1]
        ep[0:POOL_BUF] = ep[tt:tt + POOL_BUF]

    ea[CONV_A - 1:CONV_A - 1 + tt] = xa_ref[...].reshape(tt, nb, w)
    ep[POOL_BUF:POOL_BUF + tt] = xp_ref[...].reshape(tt, nb, w)

    coef = -RGLRU_C * _softplus(-lam_ref[...])
    group = lax.broadcasted_iota(jnp.int32, (1, 1, w), 2) // POOL_GROUP
    rows = cs * nb

    def chunk(k, carry):
        t0 = pl.multiple_of(k * cs, cs)
        r0 = pl.multiple_of(k * rows, rows)

        xc = caw_ref[0:1, :][None] * ea[pl.ds(t0, cs)]
        for j in range(1, CONV_A):
            xc = xc + caw_ref[j:j + 1, :][None] * ea[pl.ds(t0 + j, cs)]
        xc = (xc + cab_ref[...][None]).reshape(rows, w)
        xcb = xc.astype(BF16)
        r = jax.nn.sigmoid(_dot(xcb, wa_ref[...]) + ba_ref[...])
        gi = jax.nn.sigmoid(_dot(xcb, wx_ref[...]) + bx_ref[...])
        log_a = coef * r
        a = jnp.exp(log_a)
        a3 = a.reshape(cs, nb, w)
        b3 = (jnp.sqrt(jnp.tanh(-log_a) * (a * a + 1.0)) * (gi * xc)).reshape(cs, nb, w)
        h = h_sc[...]
        hs = []
        for t in range(cs):
            h = a3[t] * h + b3[t]
            hs.append(h[None])
        h_sc[...] = h
        hseq = jnp.concatenate(hs, axis=0).reshape(rows, w)
        y_ref[pl.ds(r0, rows), 0:w] = hseq * jax.nn.gelu(ga_ref[pl.ds(r0, rows), :])

        u = gc_ref[pl.ds(r0, rows), :] * xb_ref[pl.ds(r0, rows), :]
        eb[pl.ds(t0 + CONV_B - 1, cs)] = u.reshape(cs, nb, w)
        cb = cbw_ref[0:1, :][None] * eb[pl.ds(t0, cs)]
        for j in range(1, CONV_B):
            cb = cb + cbw_ref[j:j + 1, :][None] * eb[pl.ds(t0 + j, cs)]
        y_ref[pl.ds(r0, rows), w:2 * w] = gb_ref[pl.ds(r0, rows), :] * cb.reshape(rows, w)

        e = ep[pl.ds(t0, cs + POOL_BUF)]
        s2 = e[1:] + e[:-1]
        s4 = s2[2:] + s2[:-2]
        s8 = s4[4:] + s4[:-4]
        s16 = s8[8:] + s8[:-8]
        win = jnp.where(group == 0, s2[14:], jnp.where(group == 1, s4[12:], jnp.where(group == 2, s8[8:], s16)))
        d = (win / cnt_ref[pl.ds(t0, cs)] - e[POOL_BUF:]).reshape(rows, w)
        y_ref[pl.ds(r0, rows), 2 * w:3 * w] = _dot(d.astype(BF16), wp_ref[...]) * ps_ref[...]
        return carry

    lax.fori_loop(0, tt // cs, chunk, 0)

    ho_ref[...] = h_sc[...]
    ao_ref[...] = ea[tt:tt + CONV_A - 1]
    bo_ref[...] = eb[tt:tt + CONV_B - 1]
    po_ref[...] = ep[tt:tt + POOL_BUF]


def _seqmix_call(proj, cnt, h0, bufa, bufb, bufp, wts, nb, tt, cs):
    n = proj.shape[0]
    t_total = n // nb
    w = W_GROUP
    assert t_total % tt == 0 and tt % cs == 0 and (tt >= POOL_BUF or tt == t_total)
    col = lambda c: pl.BlockSpec((tt * nb, w), lambda i, c=c: (i, c))
    full = lambda a: pl.BlockSpec(a.shape, lambda i: (0,) * a.ndim)
    state_shapes = [jax.ShapeDtypeStruct(a.shape, F32) for a in (h0, bufa, bufb, bufp)]
    return pl.pallas_call(
        functools.partial(_seqmix_kernel, tt=tt, nb=nb, cs=cs), grid=(t_total // tt,),
        in_specs=[col(0), col(1), col(2), col(3), col(4), col(5),
                  pl.BlockSpec((tt, 1, w), lambda i: (i, 0, 0)),
                  full(h0), full(bufa), full(bufb), full(bufp)] + [full(a) for a in wts],
        out_specs=[pl.BlockSpec((tt * nb, 3 * w), lambda i: (i, 0))] + [full(a) for a in state_shapes],
        out_shape=[jax.ShapeDtypeStruct((n, 3 * w), F32)] + state_shapes,
        scratch_shapes=[pltpu.VMEM((tt + CONV_A - 1, nb, w), F32), pltpu.VMEM((tt + CONV_B - 1, nb, w), F32),
                        pltpu.VMEM((tt + POOL_BUF, nb, w), F32), pltpu.VMEM((nb, w), F32)],
        compiler_params=_cparams("arbitrary"), name="seq_mixers",
    )(proj, proj, proj, proj, proj, proj, cnt, h0, bufa, bufb, bufp, *wts)


def _attn_kernel(bias_ref, pt_ref, q_ref, k_ref, v_ref, tri_ref, qkvs_ref, trip_ref, ckt_ref, cvt_ref,
                 o_ref, os_ref, kb, vb, acc_sc, c_sc, kpage, vpage, kt_sc, vt_sc, sem, *, bq, layer, nact):
    i = pl.program_id(1)
    hd = HEAD_DIM
    t = k_ref.shape[0]
    heads = range(N_HEADS)
    pad = kb.shape[2] - hd - 2

    step = pl.program_id(0) * pl.num_programs(1) + i
    nsamp, tq, _ = qkvs_ref.shape
    npg = kpage.shape[1] // nsamp

    def page_copies(st, slot):
        copies = []
        for k in range(nsamp):
            for j in range(npg):
                page = pt_ref[st * nsamp + k, j]
                copies.append(pltpu.make_async_copy(ckt_ref.at[layer, page], kpage.at[slot, k * npg + j], sem.at[slot, 0]))
                copies.append(pltpu.make_async_copy(cvt_ref.at[layer, page], vpage.at[slot, k * npg + j], sem.at[slot, 1]))
        return copies

    @pl.when(step == 0)
    def _():
        for cp in page_copies(0, 0):
            cp.start()

    @pl.when(step + 1 < nact)
    def _():
        for cp in page_copies(step + 1, (step + 1) % 2):
            cp.start()

    @pl.when(step < nact)
    def _():
        slot = step % 2
        for cp in page_copies(step, slot):
            cp.wait()
        for k in range(nsamp):
            pages = [(kpage.at[slot, k * npg + j], vpage.at[slot, k * npg + j]) for j in range(npg)]
            os_ref[k] = _paged_sample(bias_ref, qkvs_ref[k], pages, trip_ref[...], kt_sc, vt_sc)

    @pl.when(i == 0)
    def _():
        for h in heads:
            b = jnp.full((t, 1), bias_ref[h], F32)
            b_hi = b.astype(BF16)
            b_lo = (b - b_hi.astype(F32)).astype(BF16)
            kb[h] = jnp.concatenate(
                [k_ref[:, h * hd:(h + 1) * hd].astype(BF16), b_hi, b_lo, jnp.zeros((t, pad), BF16)], axis=1)
            vb[h] = v_ref[:, h * hd:(h + 1) * hd].astype(BF16)

    tri = tri_ref[...]
    q_tail = jnp.concatenate([jnp.ones((bq, 2), BF16), jnp.zeros((bq, pad), BF16)], axis=1)
    qs = [jnp.concatenate([(q_ref[:, h * hd:(h + 1) * hd] * SB_SCALE).astype(BF16), q_tail], axis=1) for h in heads]
    causal = lax.broadcasted_iota(jnp.int32, (bq, bq), 1) < lax.broadcasted_iota(jnp.int32, (bq, bq), 0)

    def sweep(blocks, first):
        rows = [pl.ds(pl.multiple_of(j * bq, bq), bq) for j, _ in blocks]
        zs = [[_dot_nt(qs[h], kb[h, r, :]) for h in heads] for r in rows]
        lss, sps, col0 = [], [], []
        for (_, mask), zb in zip(blocks, zs):
            lb, sb, cb = [], [], []
            for h in heads:
                z = zb[h]
                sp = jnp.maximum(z, 0.0) + jnp.log(1.0 + jnp.exp2(jnp.abs(z) * -LOG2E))
                lb.append(z - sp)
                if mask is not None:
                    sp = jnp.where(mask, sp, 0.0)
                sb.append(sp.astype(BF16))
                cb.append(sp[:, 0:1])
            lss.append(lb), sps.append(sb), col0.append(cb)
        later = [[_dot(sb[h], tri) for h in heads] for sb in sps]
        carry = [None if first else c_sc[h] for h in heads]
        ps = []
        for n, (_, mask) in enumerate(blocks):
            pb = []
            for h in heads:
                e = lss[n][h] - later[n][h]
                if carry[h] is not None:
                    e = e - carry[h]
                p = jnp.exp(e)
                if mask is not None:
                    p = jnp.where(mask, p, 0.0)
                pb.append(p.astype(BF16))
                total = later[n][h][:, 0:1] + col0[n][h]
                carry[h] = total if carry[h] is None else carry[h] + total
            ps.append(pb)
        for h in heads:
            c_sc[h] = carry[h]
            pv = _dot(ps[0][h], vb[h, rows[0], :])
            for n in range(1, len(blocks)):
                pv = pv + _dot(ps[n][h], vb[h, rows[n], :])
            acc_sc[h] = pv if first else acc_sc[h] + pv

    @pl.when(i % 2 == 0)
    def _():
        sweep([(i, causal)], True)

    @pl.when(i % 2 == 1)
    def _():
        sweep([(i, causal), (i - 1, None)], True)

    top = i - 1 - i % 2

    def body(pair, carry):
        j = top - 2 * pair
        sweep([(j, None), (j - 1, None)], False)
        return carry

    lax.fori_loop(0, i // 2, body, 0)
    o_ref[...] = jnp.concatenate([acc_sc[h] for h in heads], axis=-1)


def _attn_call(qkv, qkv_s, cache_kt, cache_vt, page_table, sb_bias, tri, tri_page, layer, bq):
    nb, t, _ = qkv.shape
    ns, tq, _ = qkv_s.shape
    npg = page_table.shape[1]
    w = W_GROUP
    nsteps = nb * (t // bq)
    nsamp = -(-ns // nsteps)
    nact = ns // nsamp
    assert ns % nsamp == 0 and (N_HEADS * tq) % 8 == 0
    sample_block = lambda b, i, s, p: (jnp.minimum(b * (t // bq) + i, nact - 1), 0, 0)
    page_ring = pltpu.VMEM((2, nsamp * npg, N_HEADS, HEAD_DIM, PAGE_SIZE), F32)
    grid_spec = pltpu.PrefetchScalarGridSpec(
        num_scalar_prefetch=2, grid=(nb, t // bq),
        in_specs=[pl.BlockSpec((None, bq, w), lambda b, i, s, p: (b, i, 0)),
                  pl.BlockSpec((None, t, w), lambda b, i, s, p: (b, 0, 1)),
                  pl.BlockSpec((None, t, w), lambda b, i, s, p: (b, 0, 2)),
                  pl.BlockSpec((bq, bq), lambda b, i, s, p: (0, 0)),
                  pl.BlockSpec((nsamp, tq, 3 * w), sample_block),
                  pl.BlockSpec((PAGE_SIZE, PAGE_SIZE), lambda b, i, s, p: (0, 0)),
                  pl.BlockSpec(memory_space=pl.ANY), pl.BlockSpec(memory_space=pl.ANY)],
        out_specs=[pl.BlockSpec((None, bq, w), lambda b, i, s, p: (b, i, 0)),
                   pl.BlockSpec((nsamp, tq, w), sample_block)],
        scratch_shapes=[pltpu.VMEM((N_HEADS, t, LANES), BF16), pltpu.VMEM((N_HEADS, t, HEAD_DIM), BF16),
                        pltpu.VMEM((N_HEADS, bq, HEAD_DIM), F32), pltpu.VMEM((N_HEADS, bq, 1), F32),
                        page_ring, page_ring,
                        pltpu.VMEM((w, npg * PAGE_SIZE), BF16), pltpu.VMEM((w, npg * PAGE_SIZE), BF16),
                        pltpu.SemaphoreType.DMA((2, 2))])
    return pl.pallas_call(
        functools.partial(_attn_kernel, bq=bq, layer=layer, nact=nact), grid_spec=grid_spec,
        out_shape=[jax.ShapeDtypeStruct((nb, t, w), F32), jax.ShapeDtypeStruct((ns, tq, w), F32)],
        compiler_params=_cparams("arbitrary", "arbitrary"), name="sb_attention",
    )(sb_bias, page_table, qkv, qkv, qkv, tri, qkv_s, tri_page, cache_kt, cache_vt)


def _paged_sample(bias_ref, qkv, pages, tri, kt_sc, vt_sc):
    npg = len(pages)
    tq = qkv.shape[0]
    hd, w, ps = HEAD_DIM, W_GROUP, PAGE_SIZE
    nr = N_HEADS * tq
    for j, (k_ref, v_ref) in enumerate(pages):
        kt_sc[:, j * ps:(j + 1) * ps] = k_ref[...].reshape(w, ps).astype(BF16)
        vt_sc[:, j * ps:(j + 1) * ps] = v_ref[...].reshape(w, ps).astype(BF16)

    q, kn, vn = qkv[:, 0:w], qkv[:, w:2 * w], qkv[:, 2 * w:3 * w]
    lane_head = lax.broadcasted_iota(jnp.int32, (1, w), 1) // hd
    row = lax.broadcasted_iota(jnp.int32, (nr, 1), 0)
    row_head, row_t = row // tq, row % tq
    qbd = jnp.concatenate([jnp.where(lane_head == h, q, 0.0) for h in range(N_HEADS)], axis=0)
    bias = jnp.zeros((nr, 1), F32)
    for h in range(N_HEADS):
        bias = jnp.where(row_head == h, bias_ref[h], bias)

    run = jnp.zeros((nr, 1), F32)
    acc = jnp.zeros((nr, w), F32)
    for j in range(tq - 1, -1, -1):
        z = jnp.sum(qbd * kn[j:j + 1, :], axis=-1, keepdims=True) * SB_SCALE + bias
        seen = j < row_t
        run = run + jnp.where(seen, _softplus(z), 0.0)
        acc = acc + jnp.where(seen, jnp.exp(z - run), 0.0) * vn[j:j + 1, :]

    z_all = _dot(qbd.astype(BF16), kt_sc[...]) * SB_SCALE + bias
    z2 = jnp.concatenate([z_all[:, j * ps:(j + 1) * ps] for j in range(npg)], axis=0)
    in_page = _suffix_sum(_softplus(z2), tri)
    later = [None] * npg
    for j in range(npg - 1, -1, -1):
        later[j] = run
        run = run + in_page[j * nr:(j + 1) * nr, 0:1]
    p2 = jnp.exp(z2 - (in_page + jnp.concatenate(later, axis=0)))
    p_all = jnp.concatenate([p2[j * nr:(j + 1) * nr] for j in range(npg)], axis=1).astype(BF16)
    acc = acc + _dot_nt(p_all, vt_sc[...])
    out = jnp.where(lane_head == 0, acc[0:tq], 0.0)
    for h in range(1, N_HEADS):
        out = jnp.where(lane_head == h, acc[h * tq:(h + 1) * tq], out)
    return out


def _block_diag(wt):
    nh, d, _ = wt.shape
    return jnp.einsum("hij,hg->higj", wt, jnp.eye(nh, dtype=wt.dtype)).reshape(nh * d, nh * d)


def _window_counts(p0, t):
    win = jnp.repeat(jnp.asarray(POOL_WINDOWS, jnp.int32), POOL_GROUP)
    pos = p0 + jnp.arange(t, dtype=jnp.int32)
    return jnp.minimum(win[None, :], pos[:, None] + 1).astype(F32).reshape(t, 1, W_GROUP)


def _to_time_major(a):
    return jnp.swapaxes(a, 0, 1)


def kernel(x_prompt, x_sample, cache_k, cache_v, state_rglru_h, state_conv_a, state_conv_b, state_pool, page_table,
           ln_in_g, ln_in_b, w_in, conv_a_w, conv_a_b, lru_wa, lru_ba, lru_wx, lru_bx, lru_lambda, conv_b_w,
           sb_bias, pool_w, pool_scale, w_out, ln1_g, ln1_b, w_ff1, w_ff2, ln2_g, ln2_b):
    depth = w_in.shape[0]
    alpha = (2.0 * depth) ** 0.25
    bp, tp, d = x_prompt.shape
    bs, ts, _ = x_sample.shape
    past_len = page_table.shape[1] * PAGE_SIZE
    w = W_GROUP

    tm_p = min(512, bp * tp)
    tm_s = min(512, bs * ts)
    bq = min(256, tp)
    tt_p = min(128, tp)
    cs_p = min(16, tt_p)

    cnt_p = _window_counts(0, tp)
    cnt_s = _window_counts(past_len, ts)
    tri_q = jnp.tril(jnp.ones((bq, bq), BF16), k=-1)
    tri_page = jnp.tril(jnp.ones((PAGE_SIZE, PAGE_SIZE), BF16))
    cache_kt = jnp.transpose(cache_k, (0, 1, 3, 4, 2))
    cache_vt = jnp.transpose(cache_v, (0, 1, 3, 4, 2))
    zeros = lambda *shape: jnp.zeros(shape, F32)
    row = lambda a: a.reshape(1, -1)

    nmix = (N_PROJ_GROUPS - 3) * w
    w_in_b = jnp.concatenate([w_in[..., :5 * w], w_in[..., 8 * w:], w_in[..., 5 * w:8 * w]], axis=-1).astype(BF16)
    stack_row = lambda a: a.reshape(depth, 1, -1)
    dense_w = (jnp.concatenate([w_out[:, :2 * w], w_out[:, 3 * w:]], axis=1).astype(BF16),
               w_out[:, 2 * w:3 * w].astype(BF16), stack_row(ln1_g), stack_row(ln1_b),
               w_ff1.astype(BF16), w_ff2.astype(BF16), stack_row(ln2_g), stack_row(ln2_b), w_in_b)

    xp, mix_p, qkv_p, k_stack, v_stack = _ln_inproj_call(x_prompt, ln_in_g, ln_in_b, w_in_b, nmix, bp, tp, tm_p, True)
    xs, mix_s, qkv_s = _ln_inproj_call(
        _to_time_major(x_sample).reshape(ts * bs, d), ln_in_g, ln_in_b, w_in_b, nmix, bs, ts, tm_s, False)

    outs = {name: [] for name in ("ks", "vs", "hp", "hs", "ap", "as", "bp", "bs", "pp", "ps")}
    for l in range(depth):
        mix_w = (conv_a_w[l], row(conv_a_b[l]), _block_diag(lru_wa[l]).astype(BF16), row(lru_ba[l]),
                 _block_diag(lru_wx[l]).astype(BF16), row(lru_bx[l]), row(lru_lambda[l]), conv_b_w[l],
                 _block_diag(pool_w[l]).astype(BF16), row(pool_scale[l]))

        yabd_p, h_n, a_n, b_n, p_n = _seqmix_call(
            mix_p, cnt_p, zeros(bp, w), zeros(CONV_A - 1, bp, w), zeros(CONV_B - 1, bp, w), zeros(POOL_BUF, bp, w),
            mix_w, bp, tt_p, cs_p)
        outs["hp"].append(h_n)
        outs["ap"].append(_to_time_major(a_n))
        outs["bp"].append(_to_time_major(b_n))
        outs["pp"].append(_to_time_major(p_n))
        yabd, h_n, a_n, b_n, p_n = _seqmix_call(
            mix_s, cnt_s, state_rglru_h[l], _to_time_major(state_conv_a[l]), _to_time_major(state_conv_b[l]),
            _to_time_major(state_pool[l]), mix_w, bs, ts, ts)

        qkv = _to_time_major(qkv_s.reshape(ts, bs, 3 * w))
        yc_p, yc = _attn_call(qkv_p, qkv, cache_kt, cache_vt, page_table, sb_bias[l], tri_q, tri_page, l, bq)
        yc = _to_time_major(yc).reshape(ts * bs, w)

        res = _dense_call(yabd_p, yc_p, xp, dense_w, (k_stack, v_stack), l, nmix, bp, tp, tm_p, alpha, True)
        if l == depth - 1:
            y_prompt = res
        else:
            xp, mix_p, qkv_p, k_stack, v_stack = res
        res = _dense_call(yabd, yc, xs, dense_w, None, l, nmix, bs, ts, tm_s, alpha, False)
        if l == depth - 1:
            y_sample = _to_time_major(res.reshape(ts, bs, d))
        else:
            xs, mix_s, qkv_s = res
        outs["ks"].append(qkv[..., w:2 * w].reshape(bs, ts, N_HEADS, HEAD_DIM))
        outs["vs"].append(qkv[..., 2 * w:].reshape(bs, ts, N_HEADS, HEAD_DIM))
        outs["hs"].append(h_n)
        outs["as"].append(_to_time_major(a_n))
        outs["bs"].append(_to_time_major(b_n))
        outs["ps"].append(_to_time_major(p_n))

    st = {name: jnp.stack(v) for name, v in outs.items()}
    heads = lambda a: a.reshape(depth, bp, tp, N_HEADS, HEAD_DIM)
    return (y_prompt, y_sample, heads(k_stack), heads(v_stack), st["ks"], st["vs"], st["hp"], st["hs"],
            st["ap"], st["as"], st["bp"], st["bs"], st["pp"], st["ps"])
```

```python
import functools

import jax
import jax.numpy as jnp
from jax import lax
from jax.experimental import pallas as pl
from jax.experimental.pallas import tpu as pltpu

F32 = jnp.float32
BF16 = jnp.bfloat16

N_HEADS = 4
W_GROUP = 256
HEAD_DIM = W_GROUP // N_HEADS
N_PROJ_GROUPS = 9
CONV_A = 4
CONV_B = 3
RGLRU_C = 8.0
POOL_WINDOWS = (2, 4, 8, 16)
POOL_BUF = max(POOL_WINDOWS) - 1
POOL_GROUP = W_GROUP // len(POOL_WINDOWS)
PAGE_SIZE = 128
LN_EPS = 1e-5
SB_SCALE = HEAD_DIM ** -0.5
LANES = 128
LOG2E = 1.4426950408889634
QUERY_SPLIT = 1

VMEM_LIMIT_BYTES = 56 * 1024 * 1024


def _cparams(*sem):
    return pltpu.CompilerParams(dimension_semantics=sem, vmem_limit_bytes=VMEM_LIMIT_BYTES)


def _layer_norm(x, g, b):
    mu = jnp.mean(x, axis=-1, keepdims=True)
    xc = x - mu
    var = jnp.mean(xc * xc, axis=-1, keepdims=True)
    return xc * lax.rsqrt(var + LN_EPS) * g + b


def _softplus(x):
    return jnp.maximum(x, 0.0) + jnp.log1p(jnp.exp(-jnp.abs(x)))


def _dot(a, b):
    return jnp.dot(a, b, preferred_element_type=F32)


def _dot_nt(a, b):
    return lax.dot_general(a, b, (((1,), (1,)), ((), ())), preferred_element_type=F32)


def _suffix_sum(x, tri):
    hi = x.astype(BF16)
    lo = (x - hi.astype(F32)).astype(BF16)
    return _dot(hi, tri) + _dot(lo, tri)


def _rows_from_blocks(src_ref, slab_ref, fn=lambda v: v):
    nb, tt, c = src_ref.shape
    for s in range(nb):
        v = fn(src_ref[s])
        for j in range(c // LANES):
            slab_ref[j, pl.ds(s, tt, stride=nb), :] = v[:, j * LANES:(j + 1) * LANES]
    return jnp.concatenate([slab_ref[j] for j in range(c // LANES)], axis=1)


def _blocks_from_rows(val, slab_ref, dst_ref, copies=()):
    nb, tt, c = dst_ref.shape
    for j in range(c // LANES):
        slab_ref[j] = val[:, j * LANES:(j + 1) * LANES]
    for s in range(nb):
        for j in range(c // LANES):
            piece = slab_ref[j, pl.ds(s, tt, stride=nb), :]
            dst_ref[s, :, j * LANES:(j + 1) * LANES] = piece
            for ref, c0 in copies:
                if c0 <= j * LANES < c0 + ref.shape[-1]:
                    ref[s, :, j * LANES - c0:(j + 1) * LANES - c0] = piece


def _emit_inproj(xb, w_ref, mix_ref, qkv_ref, kv_refs, slab_ref):
    nmix = mix_ref.shape[-1]
    mix_ref[...] = _dot(xb, w_ref[:, :nmix])
    if slab_ref is None:
        qkv_ref[...] = _dot(xb, w_ref[:, nmix:])
    else:
        k_ref, v_ref = kv_refs
        _blocks_from_rows(_dot(xb, w_ref[:, nmix:]), slab_ref, qkv_ref, ((k_ref, W_GROUP), (v_ref, 2 * W_GROUP)))


def _ln_inproj_kernel(x_ref, g_ref, b_ref, w_ref, *rest, batch_major):
    norm = lambda v: _layer_norm(v, g_ref[...], b_ref[...])
    if batch_major:
        xo_ref, mix_ref, qkv_ref, k_ref, v_ref, slab_x, slab_qkv = rest[2:]
        x0 = _rows_from_blocks(x_ref, slab_x, norm)
        xo_ref[...] = x0
        _emit_inproj(x0.astype(BF16), w_ref, mix_ref, qkv_ref, (k_ref, v_ref), slab_qkv)
    else:
        xo_ref, mix_ref, qkv_ref = rest
        x0 = norm(x_ref[...])
        xo_ref[...] = x0
        _emit_inproj(x0.astype(BF16), w_ref, mix_ref, qkv_ref, None, None)


def _dense_kernel(yabd_ref, yc_ref, x_ref, wabd_ref, wc_ref, g1_ref, b1_ref, w1_ref, w2_ref, g2_ref, b2_ref,
                  *rest, alpha, fc, batch_major, last):
    kv_refs = None
    if last:
        (y_ref, acc_ref), scratch = rest[:2], rest[2:]
    elif batch_major:
        win_ref, xo_ref, mix_ref, qkv_ref, k_ref, v_ref, acc_ref = rest[:1] + rest[3:9]
        kv_refs, scratch = (k_ref, v_ref), rest[9:]
    else:
        (win_ref, xo_ref, mix_ref, qkv_ref, acc_ref), scratch = rest[:5], rest[5:]
    yc = _rows_from_blocks(yc_ref, scratch[0]) if batch_major else yc_ref[...]
    mix = _dot(yabd_ref[...].astype(BF16), wabd_ref[...]) + _dot(yc.astype(BF16), wc_ref[...])
    x1 = _layer_norm(alpha * x_ref[...] + mix, g1_ref[...], b1_ref[...])
    xb = x1.astype(BF16)
    for c in range(w1_ref.shape[1] // fc):
        h = jnp.maximum(_dot(xb, w1_ref[:, c * fc:(c + 1) * fc]), 0.0)
        part = _dot((h * h).astype(BF16), w2_ref[c * fc:(c + 1) * fc, :])
        if c == 0:
            acc_ref[...] = part
        else:
            acc_ref[...] += part
    x2 = _layer_norm(alpha * x1 + acc_ref[...], g2_ref[...], b2_ref[...])
    if not last:
        xo_ref[...] = x2
        _emit_inproj(x2.astype(BF16), win_ref, mix_ref, qkv_ref, kv_refs, scratch[1] if batch_major else None)
    elif batch_major:
        _blocks_from_rows(x2, scratch[1], y_ref)
    else:
        y_ref[...] = x2


def _resident(shape, index_map):
    return pl.BlockSpec(shape, index_map, pipeline_mode=pl.Buffered(1))


def _layer_block(a, layer):
    return _resident((None,) + a.shape[1:], lambda i: (layer, 0, 0))


def _slab(rows, c):
    return pltpu.VMEM((c // LANES, rows, LANES), F32)


def _row_specs(nb, t, tm, batch_major):
    rows = lambda c: (pl.BlockSpec((tm, c), lambda i: (i, 0)), (nb * t, c))
    if batch_major:
        blocks = lambda c: (pl.BlockSpec((nb, tm // nb, c), lambda i: (0, i, 0)), (nb, t, c))
    else:
        blocks = rows
    return rows, blocks


def _kv_stack_spec(depth, nb, t, tm, layer):
    return (pl.BlockSpec((None, nb, tm // nb, W_GROUP), lambda i: (layer, 0, i, 0)), (depth, nb, t, W_GROUP))


def _ln_inproj_call(x, g, b, w_in, nmix, nb, t, tm, batch_major):
    d = x.shape[-1]
    depth = w_in.shape[0]
    nqkv = w_in.shape[2] - nmix
    rows, blocks = _row_specs(nb, t, tm, batch_major)
    vec = _resident((1, d), lambda i: (0, 0))
    in_specs = [blocks(d)[0], vec, vec, _layer_block(w_in, 0)]
    operands = (x, g.reshape(1, d), b.reshape(1, d), w_in)
    out = [rows(d), rows(nmix), blocks(nqkv)]
    aliases = {}
    if batch_major:
        aliases = {len(operands): 3, len(operands) + 1: 4}
        in_specs += [pl.BlockSpec(memory_space=pl.ANY)] * 2
        operands += (jnp.zeros((depth, nb, t, W_GROUP), F32), jnp.zeros((depth, nb, t, W_GROUP), F32))
        out += [_kv_stack_spec(depth, nb, t, tm, 0)] * 2
    return pl.pallas_call(
        functools.partial(_ln_inproj_kernel, batch_major=batch_major), grid=(nb * t // tm,),
        in_specs=in_specs, out_specs=[s for s, _ in out],
        out_shape=[jax.ShapeDtypeStruct(shp, F32) for _, shp in out],
        scratch_shapes=[_slab(tm, d), _slab(tm, nqkv)] if batch_major else [],
        input_output_aliases=aliases, compiler_params=_cparams("arbitrary"), name="ln_in_proj")(*operands)


def _dense_call(yabd, yc, x, wts, kv_stacks, layer, nmix, nb, t, tm, alpha, batch_major, fc=512):
    d = x.shape[-1]
    w_in = wts[-1]
    depth = w_in.shape[0]
    last = layer == depth - 1
    nqkv = w_in.shape[2] - nmix
    rows, blocks = _row_specs(nb, t, tm, batch_major)
    in_specs = [rows(yabd.shape[-1])[0], blocks(yc.shape[-1])[0], rows(d)[0]] + [_layer_block(a, layer) for a in wts[:-1]]
    scratch = [pltpu.VMEM((tm, d), F32)]
    aliases = {}
    if batch_major:
        scratch += [_slab(tm, yc.shape[-1]), _slab(tm, d if last else nqkv)]
    if last:
        out = [blocks(d)]
        operands = (yabd, yc, x) + tuple(wts[:-1])
    else:
        in_specs.append(_layer_block(w_in, layer + 1))
        out = [rows(d), rows(nmix), blocks(nqkv)]
        operands = (yabd, yc, x) + tuple(wts)
        if batch_major:
            aliases = {len(operands): 3, len(operands) + 1: 4}
            in_specs += [pl.BlockSpec(memory_space=pl.ANY)] * 2
            out += [_kv_stack_spec(depth, nb, t, tm, layer + 1)] * 2
            operands += tuple(kv_stacks)
    res = pl.pallas_call(
        functools.partial(_dense_kernel, alpha=alpha, fc=fc, batch_major=batch_major, last=last),
        grid=(nb * t // tm,), in_specs=in_specs,
        out_specs=[s for s, _ in out], out_shape=[jax.ShapeDtypeStruct(shp, F32) for _, shp in out],
        scratch_shapes=scratch, input_output_aliases=aliases,
        compiler_params=_cparams("arbitrary"), name="dense_layer")(*operands)
    return res[0] if last else res


def _seqmix_kernel(xa_ref, ga_ref, xb_ref, gb_ref, gc_ref, xp_ref, cnt_ref,
                   h0_ref, bufa_ref, bufb_ref, bufp_ref,
                   caw_ref, cab_ref, wa_ref, ba_ref, wx_ref, bx_ref, lam_ref, cbw_ref, wp_ref, ps_ref,
                   y_ref, ho_ref, ao_ref, bo_ref, po_ref,
                   ea, eb, ep, h_sc, *, tt, nb, cs):
    w = W_GROUP
    i = pl.program_id(0)

    @pl.when(i == 0)
    def _():
        ea[0:CONV_A - 1] = bufa_ref[...]
        eb[0:CONV_B - 1] = bufb_ref[...]
        ep[0:POOL_BUF] = bufp_ref[...]
        h_sc[...] = h0_ref[...]

    @pl.when(i > 0)
    def _():
        ea[0:CONV_A - 1] = ea[tt:tt + CONV_A - 1]
        eb[0:CONV_B - 1] = eb[tt:tt + CONV_B - 1]
        ep[0:POOL_BUF] = ep[tt:tt + POOL_BUF]

    ea[CONV_A - 1:CONV_A - 1 + tt] = xa_ref[...].reshape(tt, nb, w)
    ep[POOL_BUF:POOL_BUF + tt] = xp_ref[...].reshape(tt, nb, w)

    coef = -RGLRU_C * _softplus(-lam_ref[...])
    group = lax.broadcasted_iota(jnp.int32, (1, 1, w), 2) // POOL_GROUP
    rows = cs * nb

    def chunk(k, carry):
        t0 = pl.multiple_of(k * cs, cs)
        r0 = pl.multiple_of(k * rows, rows)

        xc = caw_ref[0:1, :][None] * ea[pl.ds(t0, cs)]
        for j in range(1, CONV_A):
            xc = xc + caw_ref[j:j + 1, :][None] * ea[pl.ds(t0 + j, cs)]
        xc = (xc + cab_ref[...][None]).reshape(rows, w)
        xcb = xc.astype(BF16)
        r = jax.nn.sigmoid(_dot(xcb, wa_ref[...]) + ba_ref[...])
        gi = jax.nn.sigmoid(_dot(xcb, wx_ref[...]) + bx_ref[...])
        log_a = coef * r
        a = jnp.exp(log_a)
        a3 = a.reshape(cs, nb, w)
        b3 = (jnp.sqrt(jnp.tanh(-log_a) * (a * a + 1.0)) * (gi * xc)).reshape(cs, nb, w)
        h = h_sc[...]
        hs = []
        for t in range(cs):
            h = a3[t] * h + b3[t]
            hs.append(h[None])
        h_sc[...] = h
        hseq = jnp.concatenate(hs, axis=0).reshape(rows, w)
        y_ref[pl.ds(r0, rows), 0:w] = hseq * jax.nn.gelu(ga_ref[pl.ds(r0, rows), :])

        u = gc_ref[pl.ds(r0, rows), :] * xb_ref[pl.ds(r0, rows), :]
        eb[pl.ds(t0 + CONV_B - 1, cs)] = u.reshape(cs, nb, w)
        cb = cbw_ref[0:1, :][None] * eb[pl.ds(t0, cs)]
        for j in range(1, CONV_B):
            cb = cb + cbw_ref[j:j + 1, :][None] * eb[pl.ds(t0 + j, cs)]
        y_ref[pl.ds(r0, rows), w:2 * w] = gb_ref[pl.ds(r0, rows), :] * cb.reshape(rows, w)

        e = ep[pl.ds(t0, cs + POOL_BUF)]
        s2 = e[1:] + e[:-1]
        s4 = s2[2:] + s2[:-2]
        s8 = s4[4:] + s4[:-4]
        s16 = s8[8:] + s8[:-8]
        win = jnp.where(group == 0, s2[14:], jnp.where(group == 1, s4[12:], jnp.where(group == 2, s8[8:], s16)))
        d = (win / cnt_ref[pl.ds(t0, cs)] - e[POOL_BUF:]).reshape(rows, w)
        y_ref[pl.ds(r0, rows), 2 * w:3 * w] = _dot(d.astype(BF16), wp_ref[...]) * ps_ref[...]
        return carry

    lax.fori_loop(0, tt // cs, chunk, 0)

    ho_ref[...] = h_sc[...]
    ao_ref[...] = ea[tt:tt + CONV_A - 1]
    bo_ref[...] = eb[tt:tt + CONV_B - 1]
    po_ref[...] = ep[tt:tt + POOL_BUF]


def _seqmix_call(proj, cnt, h0, bufa, bufb, bufp, wts, nb, tt, cs):
    n = proj.shape[0]
    t_total = n // nb
    w = W_GROUP
    assert t_total % tt == 0 and tt % cs == 0 and (tt >= POOL_BUF or tt == t_total)
    col = lambda c: pl.BlockSpec((tt * nb, w), lambda i, c=c: (i, c))
    full = lambda a: pl.BlockSpec(a.shape, lambda i: (0,) * a.ndim)
    state_shapes = [jax.ShapeDtypeStruct(a.shape, F32) for a in (h0, bufa, bufb, bufp)]
    return pl.pallas_call(
        functools.partial(_seqmix_kernel, tt=tt, nb=nb, cs=cs), grid=(t_total // tt,),
        in_specs=[col(0), col(1), col(2), col(3), col(4), col(5),
                  pl.BlockSpec((tt, 1, w), lambda i: (i, 0, 0)),
                  full(h0), full(bufa), full(bufb), full(bufp)] + [full(a) for a in wts],
        out_specs=[pl.BlockSpec((tt * nb, 3 * w), lambda i: (i, 0))] + [full(a) for a in state_shapes],
        out_shape=[jax.ShapeDtypeStruct((n, 3 * w), F32)] + state_shapes,
        scratch_shapes=[pltpu.VMEM((tt + CONV_A - 1, nb, w), F32), pltpu.VMEM((tt + CONV_B - 1, nb, w), F32),
                        pltpu.VMEM((tt + POOL_BUF, nb, w), F32), pltpu.VMEM((nb, w), F32)],
        compiler_params=_cparams("arbitrary"), name="seq_mixers",
    )(proj, proj, proj, proj, proj, proj, cnt, h0, bufa, bufb, bufp, *wts)


def _attn_kernel(bias_ref, pt_ref, q_ref, k_ref, v_ref, tri_ref, qkvs_ref, trip_ref, ckt_ref, cvt_ref,
                 o_ref, os_ref, kb, vb, acc_sc, c_sc, kpage, vpage, kt_sc, vt_sc, sem, *, bq, layer, nact):
    i = pl.program_id(1)
    hd = HEAD_DIM
    t = k_ref.shape[0]
    heads = range(N_HEADS)
    pad = kb.shape[2] - hd - 2

    step = pl.program_id(0) * pl.num_programs(1) + i
    nsamp, tq, _ = qkvs_ref.shape
    npg = kpage.shape[1] // nsamp

    def page_copies(st, slot):
        copies = []
        for k in range(nsamp):
            for j in range(npg):
                page = pt_ref[st * nsamp + k, j]
                copies.append(pltpu.make_async_copy(ckt_ref.at[layer, page], kpage.at[slot, k * npg + j], sem.at[slot, 0]))
                copies.append(pltpu.make_async_copy(cvt_ref.at[layer, page], vpage.at[slot, k * npg + j], sem.at[slot, 1]))
        return copies

    @pl.when(step == 0)
    def _():
        for cp in page_copies(0, 0):
            cp.start()

    @pl.when(step + 1 < nact)
    def _():
        for cp in page_copies(step + 1, (step + 1) % 2):
            cp.start()

    @pl.when(step < nact)
    def _():
        slot = step % 2
        for cp in page_copies(step, slot):
            cp.wait()
        for k in range(nsamp):
            pages = [(kpage.at[slot, k * npg + j], vpage.at[slot, k * npg + j]) for j in range(npg)]
            os_ref[k] = _paged_sample(bias_ref, qkvs_ref[k], pages, trip_ref[...], kt_sc, vt_sc)

    @pl.when(i == 0)
    def _():
        for h in heads:
            b = jnp.full((t, 1), bias_ref[h], F32)
            b_hi = b.astype(BF16)
            b_lo = (b - b_hi.astype(F32)).astype(BF16)
            kb[h] = jnp.concatenate(
                [k_ref[:, h * hd:(h + 1) * hd].astype(BF16), b_hi, b_lo, jnp.zeros((t, pad), BF16)], axis=1)
            vb[h] = v_ref[:, h * hd:(h + 1) * hd].astype(BF16)

    tri = tri_ref[...]
    q_tail = jnp.concatenate([jnp.ones((bq, 2), BF16), jnp.zeros((bq, pad), BF16)], axis=1)
    qs = [jnp.concatenate([(q_ref[:, h * hd:(h + 1) * hd] * SB_SCALE).astype(BF16), q_tail], axis=1) for h in heads]
    causal = lax.broadcasted_iota(jnp.int32, (bq, bq), 1) < lax.broadcasted_iota(jnp.int32, (bq, bq), 0)

    def sweep(blocks, first):
        rows = [pl.ds(pl.multiple_of(j * bq, bq), bq) for j, _ in blocks]
        rq = bq // QUERY_SPLIT
        units = [(n, h, r) for n in range(len(blocks)) for h in heads for r in range(QUERY_SPLIT)]
        strip = lambda r: slice(r * rq, (r + 1) * rq)
        carry = {(h, r): None if first else c_sc[h, strip(r)] for h in heads for r in range(QUERY_SPLIT)}
        pvs = dict.fromkeys(carry)
        z, ls, col0, later = {}, {}, {}, {}
        for k in range(len(units) + 2):
            if k < len(units):
                n, h, r = units[k]
                z[k] = _dot_nt(qs[h][strip(r)], kb[h, rows[n], :])
            if 1 <= k <= len(units):
                n, h, r = units[k - 1]
                mask = None if blocks[n][1] is None else blocks[n][1][strip(r)]
                zz = z.pop(k - 1)
                sp = jnp.maximum(zz, 0.0) + jnp.log(1.0 + jnp.exp2(jnp.abs(zz) * -LOG2E))
                ls[k - 1] = zz - sp
                if mask is not None:
                    sp = jnp.where(mask, sp, 0.0)
                col0[k - 1] = sp[:, 0:1]
                later[k - 1] = _dot(sp.astype(BF16), tri)
            if 2 <= k:
                n, h, r = units[k - 2]
                mask = None if blocks[n][1] is None else blocks[n][1][strip(r)]
                lt = later.pop(k - 2)
                e = ls.pop(k - 2) - lt
                if carry[h, r] is not None:
                    e = e - carry[h, r]
                p = jnp.exp(e)
                if mask is not None:
                    p = jnp.where(mask, p, 0.0)
                total = lt[:, 0:1] + col0.pop(k - 2)
                carry[h, r] = total if carry[h, r] is None else carry[h, r] + total
                pv = _dot(p.astype(BF16), vb[h, rows[n], :])
                pvs[h, r] = pv if pvs[h, r] is None else pvs[h, r] + pv
        for h, r in carry:
            c_sc[h, strip(r)] = carry[h, r]
            acc_sc[h, strip(r)] = pvs[h, r] if first else acc_sc[h, strip(r)] + pvs[h, r]

    @pl.when(i % 2 == 0)
    def _():
        sweep([(i, causal)], True)

    @pl.when(i % 2 == 1)
    def _():
        sweep([(i, causal), (i - 1, None)], True)

    top = i - 1 - i % 2

    def body(pair, carry):
        j = top - 2 * pair
        sweep([(j, None), (j - 1, None)], False)
        return carry

    lax.fori_loop(0, i // 2, body, 0)
    o_ref[...] = jnp.concatenate([acc_sc[h] for h in heads], axis=-1)


def _attn_call(qkv, qkv_s, cache_kt, cache_vt, page_table, sb_bias, tri, tri_page, layer, bq):
    nb, t, _ = qkv.shape
    ns, tq, _ = qkv_s.shape
    npg = page_table.shape[1]
    w = W_GROUP
    nsteps = nb * (t // bq)
    nsamp = -(-ns // nsteps)
    nact = ns // nsamp
    assert ns % nsamp == 0 and (N_HEADS * tq) % 8 == 0
    sample_block = lambda b, i, s, p: (jnp.minimum(b * (t // bq) + i, nact - 1), 0, 0)
    page_ring = pltpu.VMEM((2, nsamp * npg, N_HEADS, HEAD_DIM, PAGE_SIZE), F32)
    grid_spec = pltpu.PrefetchScalarGridSpec(
        num_scalar_prefetch=2, grid=(nb, t // bq),
        in_specs=[pl.BlockSpec((None, bq, w), lambda b, i, s, p: (b, i, 0)),
                  pl.BlockSpec((None, t, w), lambda b, i, s, p: (b, 0, 1)),
                  pl.BlockSpec((None, t, w), lambda b, i, s, p: (b, 0, 2)),
                  pl.BlockSpec((bq, bq), lambda b, i, s, p: (0, 0)),
                  pl.BlockSpec((nsamp, tq, 3 * w), sample_block),
                  pl.BlockSpec((PAGE_SIZE, PAGE_SIZE), lambda b, i, s, p: (0, 0)),
                  pl.BlockSpec(memory_space=pl.ANY), pl.BlockSpec(memory_space=pl.ANY)],
        out_specs=[pl.BlockSpec((None, bq, w), lambda b, i, s, p: (b, i, 0)),
                   pl.BlockSpec((nsamp, tq, w), sample_block)],
        scratch_shapes=[pltpu.VMEM((N_HEADS, t, LANES), BF16), pltpu.VMEM((N_HEADS, t, HEAD_DIM), BF16),
                        pltpu.VMEM((N_HEADS, bq, HEAD_DIM), F32), pltpu.VMEM((N_HEADS, bq, 1), F32),
                        page_ring, page_ring,
                        pltpu.VMEM((w, npg * PAGE_SIZE), BF16), pltpu.VMEM((w, npg * PAGE_SIZE), BF16),
                        pltpu.SemaphoreType.DMA((2, 2))])
    return pl.pallas_call(
        functools.partial(_attn_kernel, bq=bq, layer=layer, nact=nact), grid_spec=grid_spec,
        out_shape=[jax.ShapeDtypeStruct((nb, t, w), F32), jax.ShapeDtypeStruct((ns, tq, w), F32)],
        compiler_params=_cparams("arbitrary", "arbitrary"), name="sb_attention",
    )(sb_bias, page_table, qkv, qkv, qkv, tri, qkv_s, tri_page, cache_kt, cache_vt)


def _paged_sample(bias_ref, qkv, pages, tri, kt_sc, vt_sc):
    npg = len(pages)
    tq = qkv.shape[0]
    hd, w, ps = HEAD_DIM, W_GROUP, PAGE_SIZE
    nr = N_HEADS * tq
    for j, (k_ref, v_ref) in enumerate(pages):
        kt_sc[:, j * ps:(j + 1) * ps] = k_ref[...].reshape(w, ps).astype(BF16)
        vt_sc[:, j * ps:(j + 1) * ps] = v_ref[...].reshape(w, ps).astype(BF16)

    q, kn, vn = qkv[:, 0:w], qkv[:, w:2 * w], qkv[:, 2 * w:3 * w]
    lane_head = lax.broadcasted_iota(jnp.int32, (1, w), 1) // hd
    row = lax.broadcasted_iota(jnp.int32, (nr, 1), 0)
    row_head, row_t = row // tq, row % tq
    qbd = jnp.concatenate([jnp.where(lane_head == h, q, 0.0) for h in range(N_HEADS)], axis=0)
    bias = jnp.zeros((nr, 1), F32)
    for h in range(N_HEADS):
        bias = jnp.where(row_head == h, bias_ref[h], bias)

    run = jnp.zeros((nr, 1), F32)
    acc = jnp.zeros((nr, w), F32)
    for j in range(tq - 1, -1, -1):
        z = jnp.sum(qbd * kn[j:j + 1, :], axis=-1, keepdims=True) * SB_SCALE + bias
        seen = j < row_t
        run = run + jnp.where(seen, _softplus(z), 0.0)
        acc = acc + jnp.where(seen, jnp.exp(z - run), 0.0) * vn[j:j + 1, :]

    z_all = _dot(qbd.astype(BF16), kt_sc[...]) * SB_SCALE + bias
    z2 = jnp.concatenate([z_all[:, j * ps:(j + 1) * ps] for j in range(npg)], axis=0)
    in_page = _suffix_sum(_softplus(z2), tri)
    later = [None] * npg
    for j in range(npg - 1, -1, -1):
        later[j] = run
        run = run + in_page[j * nr:(j + 1) * nr, 0:1]
    p2 = jnp.exp(z2 - (in_page + jnp.concatenate(later, axis=0)))
    p_all = jnp.concatenate([p2[j * nr:(j + 1) * nr] for j in range(npg)], axis=1).astype(BF16)
    acc = acc + _dot_nt(p_all, vt_sc[...])
    out = jnp.where(lane_head == 0, acc[0:tq], 0.0)
    for h in range(1, N_HEADS):
        out = jnp.where(lane_head == h, acc[h * tq:(h + 1) * tq], out)
    return out


def _block_diag(wt):
    nh, d, _ = wt.shape
    return jnp.einsum("hij,hg->higj", wt, jnp.eye(nh, dtype=wt.dtype)).reshape(nh * d, nh * d)


def _window_counts(p0, t):
    win = jnp.repeat(jnp.asarray(POOL_WINDOWS, jnp.int32), POOL_GROUP)
    pos = p0 + jnp.arange(t, dtype=jnp.int32)
    return jnp.minimum(win[None, :], pos[:, None] + 1).astype(F32).reshape(t, 1, W_GROUP)


def _to_time_major(a):
    return jnp.swapaxes(a, 0, 1)


def kernel(x_prompt, x_sample, cache_k, cache_v, state_rglru_h, state_conv_a, state_conv_b, state_pool, page_table,
           ln_in_g, ln_in_b, w_in, conv_a_w, conv_a_b, lru_wa, lru_ba, lru_wx, lru_bx, lru_lambda, conv_b_w,
           sb_bias, pool_w, pool_scale, w_out, ln1_g, ln1_b, w_ff1, w_ff2, ln2_g, ln2_b):
    depth = w_in.shape[0]
    alpha = (2.0 * depth) ** 0.25
    bp, tp, d = x_prompt.shape
    bs, ts, _ = x_sample.shape
    past_len = page_table.shape[1] * PAGE_SIZE
    w = W_GROUP

    tm_p = min(512, bp * tp)
    tm_s = min(512, bs * ts)
    bq = min(256, tp)
    tt_p = min(256, tp)
    cs_p = min(16, tt_p)

    cnt_p = _window_counts(0, tp)
    cnt_s = _window_counts(past_len, ts)
    tri_q = jnp.tril(jnp.ones((bq, bq), BF16), k=-1)
    tri_page = jnp.tril(jnp.ones((PAGE_SIZE, PAGE_SIZE), BF16))
    cache_kt = jnp.transpose(cache_k, (0, 1, 3, 4, 2))
    cache_vt = jnp.transpose(cache_v, (0, 1, 3, 4, 2))
    zeros = lambda *shape: jnp.zeros(shape, F32)
    row = lambda a: a.reshape(1, -1)

    nmix = (N_PROJ_GROUPS - 3) * w
    w_in_b = jnp.concatenate([w_in[..., :5 * w], w_in[..., 8 * w:], w_in[..., 5 * w:8 * w]], axis=-1).astype(BF16)
    stack_row = lambda a: a.reshape(depth, 1, -1)
    dense_w = (jnp.concatenate([w_out[:, :2 * w], w_out[:, 3 * w:]], axis=1).astype(BF16),
               w_out[:, 2 * w:3 * w].astype(BF16), stack_row(ln1_g), stack_row(ln1_b),
               w_ff1.astype(BF16), w_ff2.astype(BF16), stack_row(ln2_g), stack_row(ln2_b), w_in_b)

    xp, mix_p, qkv_p, k_stack, v_stack = _ln_inproj_call(x_prompt, ln_in_g, ln_in_b, w_in_b, nmix, bp, tp, tm_p, True)
    xs, mix_s, qkv_s = _ln_inproj_call(
        _to_time_major(x_sample).reshape(ts * bs, d), ln_in_g, ln_in_b, w_in_b, nmix, bs, ts, tm_s, False)

    outs = {name: [] for name in ("ks", "vs", "hp", "hs", "ap", "as", "bp", "bs", "pp", "ps")}
    for l in range(depth):
        mix_w = (conv_a_w[l], row(conv_a_b[l]), _block_diag(lru_wa[l]).astype(BF16), row(lru_ba[l]),
                 _block_diag(lru_wx[l]).astype(BF16), row(lru_bx[l]), row(lru_lambda[l]), conv_b_w[l],
                 _block_diag(pool_w[l]).astype(BF16), row(pool_scale[l]))

        yabd_p, h_n, a_n, b_n, p_n = _seqmix_call(
            mix_p, cnt_p, zeros(bp, w), zeros(CONV_A - 1, bp, w), zeros(CONV_B - 1, bp, w), zeros(POOL_BUF, bp, w),
            mix_w, bp, tt_p, cs_p)
        outs["hp"].append(h_n)
        outs["ap"].append(_to_time_major(a_n))
        outs["bp"].append(_to_time_major(b_n))
        outs["pp"].append(_to_time_major(p_n))
        yabd, h_n, a_n, b_n, p_n = _seqmix_call(
            mix_s, cnt_s, state_rglru_h[l], _to_time_major(state_conv_a[l]), _to_time_major(state_conv_b[l]),
            _to_time_major(state_pool[l]), mix_w, bs, ts, 1)

        qkv = _to_time_major(qkv_s.reshape(ts, bs, 3 * w))
        yc_p, yc = _attn_call(qkv_p, qkv, cache_kt, cache_vt, page_table, sb_bias[l], tri_q, tri_page, l, bq)
        yc = _to_time_major(yc).reshape(ts * bs, w)

        res = _dense_call(yabd_p, yc_p, xp, dense_w, (k_stack, v_stack), l, nmix, bp, tp, tm_p, alpha, True)
        if l == depth - 1:
            y_prompt = res
        else:
            xp, mix_p, qkv_p, k_stack, v_stack = res
        res = _dense_call(yabd, yc, xs, dense_w, None, l, nmix, bs, ts, tm_s, alpha, False)
        if l == depth - 1:
            y_sample = _to_time_major(res.reshape(ts, bs, d))
        else:
            xs, mix_s, qkv_s = res
        outs["ks"].append(qkv[..., w:2 * w].reshape(bs, ts, N_HEADS, HEAD_DIM))
        outs["vs"].append(qkv[..., 2 * w:].reshape(bs, ts, N_HEADS, HEAD_DIM))
        outs["hs"].append(h_n)
        outs["as"].append(_to_time_major(a_n))
        outs["bs"].append(_to_time_major(b_n))
        outs["ps"].append(_to_time_major(p_n))

    st = {name: jnp.stack(v) for name, v in outs.items()}
    heads = lambda a: a.reshape(depth, bp, tp, N_HEADS, HEAD_DIM)
    return (y_prompt, y_sample, heads(k_stack), heads(v_stack), st["ks"], st["vs"], st["hp"], st["hs"],
            st["ap"], st["as"], st["bp"], st["bs"], st["pp"], st["ps"])
```

```python
import functools

import jax
import jax.numpy as jnp
from jax import lax
from jax.experimental import pallas as pl
from jax.experimental.pallas import tpu as pltpu

F32 = jnp.float32
BF16 = jnp.bfloat16

N_HEADS = 4
W_GROUP = 256
HEAD_DIM = W_GROUP // N_HEADS
N_PROJ_GROUPS = 9
CONV_A = 4
CONV_B = 3
RGLRU_C = 8.0
POOL_WINDOWS = (2, 4, 8, 16)
POOL_BUF = max(POOL_WINDOWS) - 1
POOL_GROUP = W_GROUP // len(POOL_WINDOWS)
PAGE_SIZE = 128
LN_EPS = 1e-5
SB_SCALE = HEAD_DIM ** -0.5
LANES = 128
LOG2E = 1.4426950408889634
QUERY_SPLIT = 1

VMEM_LIMIT_BYTES = 56 * 1024 * 1024


def _cparams(*sem):
    return pltpu.CompilerParams(dimension_semantics=sem, vmem_limit_bytes=VMEM_LIMIT_BYTES)


def _layer_norm(x, g, b):
    mu = jnp.mean(x, axis=-1, keepdims=True)
    xc = x - mu
    var = jnp.mean(xc * xc, axis=-1, keepdims=True)
    return xc * lax.rsqrt(var + LN_EPS) * g + b


def _softplus(x):
    return jnp.maximum(x, 0.0) + jnp.log1p(jnp.exp(-jnp.abs(x)))


def _dot(a, b):
    return jnp.dot(a, b, preferred_element_type=F32)


def _dot_nt(a, b):
    return lax.dot_general(a, b, (((1,), (1,)), ((), ())), preferred_element_type=F32)


def _suffix_sum(x, tri):
    hi = x.astype(BF16)
    lo = (x - hi.astype(F32)).astype(BF16)
    return _dot(hi, tri) + _dot(lo, tri)


def _rows_from_blocks(src_ref, slab_ref, fn=lambda v: v):
    nb, tt, c = src_ref.shape
    for s in range(nb):
        v = fn(src_ref[s])
        for j in range(c // LANES):
            slab_ref[j, pl.ds(s, tt, stride=nb), :] = v[:, j * LANES:(j + 1) * LANES]
    return jnp.concatenate([slab_ref[j] for j in range(c // LANES)], axis=1)


def _blocks_from_rows(val, slab_ref, dst_ref, copies=()):
    nb, tt, c = dst_ref.shape
    for j in range(c // LANES):
        slab_ref[j] = val[:, j * LANES:(j + 1) * LANES]
    for s in range(nb):
        for j in range(c // LANES):
            piece = slab_ref[j, pl.ds(s, tt, stride=nb), :]
            dst_ref[s, :, j * LANES:(j + 1) * LANES] = piece
            for ref, c0 in copies:
                if c0 <= j * LANES < c0 + ref.shape[-1]:
                    ref[s, :, j * LANES - c0:(j + 1) * LANES - c0] = piece


def _emit_inproj(xb, w_ref, mix_ref, qkv_ref, kv_refs, slab_ref):
    ngroups, _, w = mix_ref.shape
    nmix = ngroups * w
    mix = _dot(xb, w_ref[:, :nmix])
    for grp in range(ngroups):
        mix_ref[grp] = mix[:, grp * w:(grp + 1) * w]
    if slab_ref is None:
        qkv_ref[...] = _dot(xb, w_ref[:, nmix:])
    else:
        k_ref, v_ref = kv_refs
        _blocks_from_rows(_dot(xb, w_ref[:, nmix:]), slab_ref, qkv_ref, ((k_ref, W_GROUP), (v_ref, 2 * W_GROUP)))


def _ln_inproj_kernel(x_ref, g_ref, b_ref, w_ref, *rest, batch_major):
    norm = lambda v: _layer_norm(v, g_ref[...], b_ref[...])
    if batch_major:
        xo_ref, mix_ref, qkv_ref, k_ref, v_ref, slab_x, slab_qkv = rest[2:]
        x0 = _rows_from_blocks(x_ref, slab_x, norm)
        xo_ref[...] = x0
        _emit_inproj(x0.astype(BF16), w_ref, mix_ref, qkv_ref, (k_ref, v_ref), slab_qkv)
    else:
        xo_ref, mix_ref, qkv_ref = rest
        x0 = norm(x_ref[...])
        xo_ref[...] = x0
        _emit_inproj(x0.astype(BF16), w_ref, mix_ref, qkv_ref, None, None)


def _dense_kernel(yabd_ref, yc_ref, x_ref, wabd_ref, wc_ref, g1_ref, b1_ref, w1_ref, w2_ref, g2_ref, b2_ref,
                  *rest, alpha, fc, batch_major, last):
    kv_refs = None
    if last:
        (y_ref, acc_ref), scratch = rest[:2], rest[2:]
    elif batch_major:
        win_ref, xo_ref, mix_ref, qkv_ref, k_ref, v_ref, acc_ref = rest[:1] + rest[3:9]
        kv_refs, scratch = (k_ref, v_ref), rest[9:]
    else:
        (win_ref, xo_ref, mix_ref, qkv_ref, acc_ref), scratch = rest[:5], rest[5:]
    yc = _rows_from_blocks(yc_ref, scratch[0]) if batch_major else yc_ref[...]
    mix = _dot(yabd_ref[...].astype(BF16), wabd_ref[...]) + _dot(yc.astype(BF16), wc_ref[...])
    x1 = _layer_norm(alpha * x_ref[...] + mix, g1_ref[...], b1_ref[...])
    xb = x1.astype(BF16)
    for c in range(w1_ref.shape[1] // fc):
        h = jnp.maximum(_dot(xb, w1_ref[:, c * fc:(c + 1) * fc]), 0.0)
        part = _dot((h * h).astype(BF16), w2_ref[c * fc:(c + 1) * fc, :])
        if c == 0:
            acc_ref[...] = part
        else:
            acc_ref[...] += part
    x2 = _layer_norm(alpha * x1 + acc_ref[...], g2_ref[...], b2_ref[...])
    if not last:
        xo_ref[...] = x2
        _emit_inproj(x2.astype(BF16), win_ref, mix_ref, qkv_ref, kv_refs, scratch[1] if batch_major else None)
    elif batch_major:
        _blocks_from_rows(x2, scratch[1], y_ref)
    else:
        y_ref[...] = x2


def _resident(shape, index_map):
    return pl.BlockSpec(shape, index_map, pipeline_mode=pl.Buffered(1))


def _layer_block(a, layer):
    return _resident((None,) + a.shape[1:], lambda i: (layer, 0, 0))


def _slab(rows, c):
    return pltpu.VMEM((c // LANES, rows, LANES), F32)


def _row_specs(nb, t, tm, batch_major):
    rows = lambda c: (pl.BlockSpec((tm, c), lambda i: (i, 0)), (nb * t, c))
    if batch_major:
        blocks = lambda c: (pl.BlockSpec((nb, tm // nb, c), lambda i: (0, i, 0)), (nb, t, c))
    else:
        blocks = rows
    return rows, blocks


def _planes_spec(n, tm, c):
    return (pl.BlockSpec((c // W_GROUP, tm, W_GROUP), lambda i: (0, i, 0)), (c // W_GROUP, n, W_GROUP))


def _kv_stack_spec(depth, nb, t, tm, layer):
    return (pl.BlockSpec((None, nb, tm // nb, W_GROUP), lambda i: (layer, 0, i, 0)), (depth, nb, t, W_GROUP))


def _ln_inproj_call(x, g, b, w_in, nmix, nb, t, tm, batch_major):
    d = x.shape[-1]
    depth = w_in.shape[0]
    nqkv = w_in.shape[2] - nmix
    rows, blocks = _row_specs(nb, t, tm, batch_major)
    vec = _resident((1, d), lambda i: (0, 0))
    in_specs = [blocks(d)[0], vec, vec, _layer_block(w_in, 0)]
    operands = (x, g.reshape(1, d), b.reshape(1, d), w_in)
    out = [rows(d), _planes_spec(nb * t, tm, nmix), blocks(nqkv)]
    aliases = {}
    if batch_major:
        aliases = {len(operands): 3, len(operands) + 1: 4}
        in_specs += [pl.BlockSpec(memory_space=pl.ANY)] * 2
        operands += (jnp.zeros((depth, nb, t, W_GROUP), F32), jnp.zeros((depth, nb, t, W_GROUP), F32))
        out += [_kv_stack_spec(depth, nb, t, tm, 0)] * 2
    return pl.pallas_call(
        functools.partial(_ln_inproj_kernel, batch_major=batch_major), grid=(nb * t // tm,),
        in_specs=in_specs, out_specs=[s for s, _ in out],
        out_shape=[jax.ShapeDtypeStruct(shp, F32) for _, shp in out],
        scratch_shapes=[_slab(tm, d), _slab(tm, nqkv)] if batch_major else [],
        input_output_aliases=aliases, compiler_params=_cparams("arbitrary"), name="ln_in_proj")(*operands)


def _dense_call(yabd, yc, x, wts, kv_stacks, layer, nmix, nb, t, tm, alpha, batch_major, fc=512):
    d = x.shape[-1]
    w_in = wts[-1]
    depth = w_in.shape[0]
    last = layer == depth - 1
    nqkv = w_in.shape[2] - nmix
    rows, blocks = _row_specs(nb, t, tm, batch_major)
    in_specs = [rows(yabd.shape[-1])[0], blocks(yc.shape[-1])[0], rows(d)[0]] + [_layer_block(a, layer) for a in wts[:-1]]
    scratch = [pltpu.VMEM((tm, d), F32)]
    aliases = {}
    if batch_major:
        scratch += [_slab(tm, yc.shape[-1]), _slab(tm, d if last else nqkv)]
    if last:
        out = [blocks(d)]
        operands = (yabd, yc, x) + tuple(wts[:-1])
    else:
        in_specs.append(_layer_block(w_in, layer + 1))
        out = [rows(d), _planes_spec(nb * t, tm, nmix), blocks(nqkv)]
        operands = (yabd, yc, x) + tuple(wts)
        if batch_major:
            aliases = {len(operands): 3, len(operands) + 1: 4}
            in_specs += [pl.BlockSpec(memory_space=pl.ANY)] * 2
            out += [_kv_stack_spec(depth, nb, t, tm, layer + 1)] * 2
            operands += tuple(kv_stacks)
    res = pl.pallas_call(
        functools.partial(_dense_kernel, alpha=alpha, fc=fc, batch_major=batch_major, last=last),
        grid=(nb * t // tm,), in_specs=in_specs,
        out_specs=[s for s, _ in out], out_shape=[jax.ShapeDtypeStruct(shp, F32) for _, shp in out],
        scratch_shapes=scratch, input_output_aliases=aliases,
        compiler_params=_cparams("arbitrary"), name="dense_layer")(*operands)
    return res[0] if last else res


def _seqmix_kernel(xa_ref, ga_ref, xb_ref, gb_ref, gc_ref, xp_ref, cnt_ref,
                   h0_ref, bufa_ref, bufb_ref, bufp_ref,
                   caw_ref, cab_ref, wa_ref, ba_ref, wx_ref, bx_ref, lam_ref, cbw_ref, wp_ref, ps_ref,
                   y_ref, ho_ref, ao_ref, bo_ref, po_ref,
                   ea, eb, ep, h_sc, *, tt, nb, cs):
    w = W_GROUP
    i = pl.program_id(0)

    @pl.when(i == 0)
    def _():
        ea[0:CONV_A - 1] = bufa_ref[...]
        eb[0:CONV_B - 1] = bufb_ref[...]
        ep[0:POOL_BUF] = bufp_ref[...]
        h_sc[...] = h0_ref[...]

    @pl.when(i > 0)
    def _():
        ea[0:CONV_A - 1] = ea[tt:tt + CONV_A - 1]
        eb[0:CONV_B - 1] = eb[tt:tt + CONV_B - 1]
        ep[0:POOL_BUF] = ep[tt:tt + POOL_BUF]

    ea[CONV_A - 1:CONV_A - 1 + tt] = xa_ref[...].reshape(tt, nb, w)
    ep[POOL_BUF:POOL_BUF + tt] = xp_ref[...].reshape(tt, nb, w)

    coef = -RGLRU_C * _softplus(-lam_ref[...])
    group = lax.broadcasted_iota(jnp.int32, (1, 1, w), 2) // POOL_GROUP
    rows = cs * nb

    def chunk(k, carry):
        t0 = pl.multiple_of(k * cs, cs)
        r0 = pl.multiple_of(k * rows, rows)

        xc = caw_ref[0:1, :][None] * ea[pl.ds(t0, cs)]
        for j in range(1, CONV_A):
            xc = xc + caw_ref[j:j + 1, :][None] * ea[pl.ds(t0 + j, cs)]
        xc = (xc + cab_ref[...][None]).reshape(rows, w)
        xcb = xc.astype(BF16)
        r = jax.nn.sigmoid(_dot(xcb, wa_ref[...]) + ba_ref[...])
        gi = jax.nn.sigmoid(_dot(xcb, wx_ref[...]) + bx_ref[...])
        log_a = coef * r
        a = jnp.exp(log_a)
        a3 = a.reshape(cs, nb, w)
        b3 = (jnp.sqrt(jnp.tanh(-log_a) * (a * a + 1.0)) * (gi * xc)).reshape(cs, nb, w)
        h = h_sc[...]
        hs = []
        for t in range(cs):
            h = a3[t] * h + b3[t]
            hs.append(h[None])
        h_sc[...] = h
        hseq = jnp.concatenate(hs, axis=0).reshape(rows, w)
        y_ref[pl.ds(r0, rows), 0:w] = hseq * jax.nn.gelu(ga_ref[pl.ds(r0, rows), :])

        u = gc_ref[pl.ds(r0, rows), :] * xb_ref[pl.ds(r0, rows), :]
        eb[pl.ds(t0 + CONV_B - 1, cs)] = u.reshape(cs, nb, w)
        cb = cbw_ref[0:1, :][None] * eb[pl.ds(t0, cs)]
        for j in range(1, CONV_B):
            cb = cb + cbw_ref[j:j + 1, :][None] * eb[pl.ds(t0 + j, cs)]
        y_ref[pl.ds(r0, rows), w:2 * w] = gb_ref[pl.ds(r0, rows), :] * cb.reshape(rows, w)

        e = ep[pl.ds(t0, cs + POOL_BUF)]
        s2 = e[1:] + e[:-1]
        s4 = s2[2:] + s2[:-2]
        s8 = s4[4:] + s4[:-4]
        s16 = s8[8:] + s8[:-8]
        win = jnp.where(group == 0, s2[14:], jnp.where(group == 1, s4[12:], jnp.where(group == 2, s8[8:], s16)))
        d = (win / cnt_ref[pl.ds(t0, cs)] - e[POOL_BUF:]).reshape(rows, w)
        y_ref[pl.ds(r0, rows), 2 * w:3 * w] = _dot(d.astype(BF16), wp_ref[...]) * ps_ref[...]
        return carry

    lax.fori_loop(0, tt // cs, chunk, 0)

    ho_ref[...] = h_sc[...]
    ao_ref[...] = ea[tt:tt + CONV_A - 1]
    bo_ref[...] = eb[tt:tt + CONV_B - 1]
    po_ref[...] = ep[tt:tt + POOL_BUF]


def _seqmix_call(mix, cnt, states, state_layer, wts, layer, nb, tt, cs):
    n = mix.shape[1]
    t_total = n // nb
    w = W_GROUP
    assert t_total % tt == 0 and tt % cs == 0 and (tt >= POOL_BUF or tt == t_total)
    plane = lambda c: pl.BlockSpec((None, tt * nb, w), lambda i, c=c: (c, i, 0))
    entry = lambda a, j: pl.BlockSpec((None,) + a.shape[1:], lambda i: (j,) + (0,) * (a.ndim - 1))
    whole = lambda a: pl.BlockSpec(a.shape, lambda i: (0,) * len(a.shape))
    state_shapes = [jax.ShapeDtypeStruct(a.shape[1:], F32) for a in states]
    return pl.pallas_call(
        functools.partial(_seqmix_kernel, tt=tt, nb=nb, cs=cs), grid=(t_total // tt,),
        in_specs=[plane(c) for c in range(mix.shape[0])] + [pl.BlockSpec((tt, 1, w), lambda i: (i, 0, 0))]
        + [entry(a, state_layer) for a in states] + [entry(a, layer) for a in wts],
        out_specs=[pl.BlockSpec((tt * nb, 3 * w), lambda i: (i, 0))] + [whole(a) for a in state_shapes],
        out_shape=[jax.ShapeDtypeStruct((n, 3 * w), F32)] + state_shapes,
        scratch_shapes=[pltpu.VMEM((tt + CONV_A - 1, nb, w), F32), pltpu.VMEM((tt + CONV_B - 1, nb, w), F32),
                        pltpu.VMEM((tt + POOL_BUF, nb, w), F32), pltpu.VMEM((nb, w), F32)],
        compiler_params=_cparams("arbitrary"), name="seq_mixers",
    )(*([mix] * mix.shape[0]), cnt, *states, *wts)


def _attn_kernel(bias_ref, pt_ref, q_ref, k_ref, v_ref, tri_ref, qkvs_ref, trip_ref, ckt_ref, cvt_ref,
                 o_ref, os_ref, kb, vb, acc_sc, c_sc, kpage, vpage, kt_sc, vt_sc, sem, *, bq, layer, nact):
    i = pl.program_id(1)
    hd = HEAD_DIM
    t = k_ref.shape[0]
    heads = range(N_HEADS)
    pad = kb.shape[2] - hd - 2

    step = pl.program_id(0) * pl.num_programs(1) + i
    nsamp, tq, _ = qkvs_ref.shape
    npg = kpage.shape[1] // nsamp

    def page_copies(st, slot):
        copies = []
        for k in range(nsamp):
            for j in range(npg):
                page = pt_ref[st * nsamp + k, j]
                copies.append(pltpu.make_async_copy(ckt_ref.at[layer, page], kpage.at[slot, k * npg + j], sem.at[slot, 0]))
                copies.append(pltpu.make_async_copy(cvt_ref.at[layer, page], vpage.at[slot, k * npg + j], sem.at[slot, 1]))
        return copies

    @pl.when(step == 0)
    def _():
        for cp in page_copies(0, 0):
            cp.start()

    @pl.when(step + 1 < nact)
    def _():
        for cp in page_copies(step + 1, (step + 1) % 2):
            cp.start()

    @pl.when(step < nact)
    def _():
        slot = step % 2
        for cp in page_copies(step, slot):
            cp.wait()
        for k in range(nsamp):
            pages = [(kpage.at[slot, k * npg + j], vpage.at[slot, k * npg + j]) for j in range(npg)]
            os_ref[k] = _paged_sample(bias_ref, qkvs_ref[k], pages, trip_ref[...], kt_sc, vt_sc)

    @pl.when(i == 0)
    def _():
        for h in heads:
            b = jnp.full((t, 1), bias_ref[h], F32)
            b_hi = b.astype(BF16)
            b_lo = (b - b_hi.astype(F32)).astype(BF16)
            kb[h] = jnp.concatenate(
                [k_ref[:, h * hd:(h + 1) * hd].astype(BF16), b_hi, b_lo, jnp.zeros((t, pad), BF16)], axis=1)
            vb[h] = v_ref[:, h * hd:(h + 1) * hd].astype(BF16)

    tri = tri_ref[...]
    q_tail = jnp.concatenate([jnp.ones((bq, 2), BF16), jnp.zeros((bq, pad), BF16)], axis=1)
    qs = [jnp.concatenate([(q_ref[:, h * hd:(h + 1) * hd] * SB_SCALE).astype(BF16), q_tail], axis=1) for h in heads]
    causal = lax.broadcasted_iota(jnp.int32, (bq, bq), 1) < lax.broadcasted_iota(jnp.int32, (bq, bq), 0)

    def sweep(blocks, first):
        rows = [pl.ds(pl.multiple_of(j * bq, bq), bq) for j, _ in blocks]
        rq = bq // QUERY_SPLIT
        units = [(n, h, r) for n in range(len(blocks)) for h in heads for r in range(QUERY_SPLIT)]
        strip = lambda r: slice(r * rq, (r + 1) * rq)
        carry = {(h, r): None if first else c_sc[h, strip(r)] for h in heads for r in range(QUERY_SPLIT)}
        pvs = dict.fromkeys(carry)
        z, ls, col0, later = {}, {}, {}, {}
        for k in range(len(units) + 2):
            if k < len(units):
                n, h, r = units[k]
                z[k] = _dot_nt(qs[h][strip(r)], kb[h, rows[n], :])
            if 1 <= k <= len(units):
                n, h, r = units[k - 1]
                mask = None if blocks[n][1] is None else blocks[n][1][strip(r)]
                zz = z.pop(k - 1)
                sp = jnp.maximum(zz, 0.0) + jnp.log(1.0 + jnp.exp2(jnp.abs(zz) * -LOG2E))
                ls[k - 1] = zz - sp
                if mask is not None:
                    sp = jnp.where(mask, sp, 0.0)
                col0[k - 1] = sp[:, 0:1]
                later[k - 1] = _dot(sp.astype(BF16), tri)
            if 2 <= k:
                n, h, r = units[k - 2]
                mask = None if blocks[n][1] is None else blocks[n][1][strip(r)]
                lt = later.pop(k - 2)
                e = ls.pop(k - 2) - lt
                if carry[h, r] is not None:
                    e = e - carry[h, r]
                p = jnp.exp(e)
                if mask is not None:
                    p = jnp.where(mask, p, 0.0)
                total = lt[:, 0:1] + col0.pop(k - 2)
                carry[h, r] = total if carry[h, r] is None else carry[h, r] + total
                pv = _dot(p.astype(BF16), vb[h, rows[n], :])
                pvs[h, r] = pv if pvs[h, r] is None else pvs[h, r] + pv
        for h, r in carry:
            c_sc[h, strip(r)] = carry[h, r]
            acc_sc[h, strip(r)] = pvs[h, r] if first else acc_sc[h, strip(r)] + pvs[h, r]

    @pl.when(i % 2 == 0)
    def _():
        sweep([(i, causal)], True)

    @pl.when(i % 2 == 1)
    def _():
        sweep([(i, causal), (i - 1, None)], True)

    top = i - 1 - i % 2

    def body(pair, carry):
        j = top - 2 * pair
        sweep([(j, None), (j - 1, None)], False)
        return carry

    lax.fori_loop(0, i // 2, body, 0)
    o_ref[...] = jnp.concatenate([acc_sc[h] for h in heads], axis=-1)


def _attn_call(qkv, qkv_s, cache_kt, cache_vt, page_table, sb_bias, tri, tri_page, layer, bq):
    nb, t, _ = qkv.shape
    ns, tq, _ = qkv_s.shape
    npg = page_table.shape[1]
    w = W_GROUP
    nsteps = nb * (t // bq)
    nsamp = -(-ns // nsteps)
    nact = ns // nsamp
    assert ns % nsamp == 0 and (N_HEADS * tq) % 8 == 0
    sample_block = lambda b, i, s, p: (jnp.minimum(b * (t // bq) + i, nact - 1), 0, 0)
    page_ring = pltpu.VMEM((2, nsamp * npg, N_HEADS, HEAD_DIM, PAGE_SIZE), F32)
    grid_spec = pltpu.PrefetchScalarGridSpec(
        num_scalar_prefetch=2, grid=(nb, t // bq),
        in_specs=[pl.BlockSpec((None, bq, w), lambda b, i, s, p: (b, i, 0)),
                  pl.BlockSpec((None, t, w), lambda b, i, s, p: (b, 0, 1)),
                  pl.BlockSpec((None, t, w), lambda b, i, s, p: (b, 0, 2)),
                  pl.BlockSpec((bq, bq), lambda b, i, s, p: (0, 0)),
                  pl.BlockSpec((nsamp, tq, 3 * w), sample_block),
                  pl.BlockSpec((PAGE_SIZE, PAGE_SIZE), lambda b, i, s, p: (0, 0)),
                  pl.BlockSpec(memory_space=pl.ANY), pl.BlockSpec(memory_space=pl.ANY)],
        out_specs=[pl.BlockSpec((None, bq, w), lambda b, i, s, p: (b, i, 0)),
                   pl.BlockSpec((nsamp, tq, w), sample_block)],
        scratch_shapes=[pltpu.VMEM((N_HEADS, t, LANES), BF16), pltpu.VMEM((N_HEADS, t, HEAD_DIM), BF16),
                        pltpu.VMEM((N_HEADS, bq, HEAD_DIM), F32), pltpu.VMEM((N_HEADS, bq, 1), F32),
                        page_ring, page_ring,
                        pltpu.VMEM((w, npg * PAGE_SIZE), BF16), pltpu.VMEM((w, npg * PAGE_SIZE), BF16),
                        pltpu.SemaphoreType.DMA((2, 2))])
    return pl.pallas_call(
        functools.partial(_attn_kernel, bq=bq, layer=layer, nact=nact), grid_spec=grid_spec,
        out_shape=[jax.ShapeDtypeStruct((nb, t, w), F32), jax.ShapeDtypeStruct((ns, tq, w), F32)],
        compiler_params=_cparams("arbitrary", "arbitrary"), name="sb_attention",
    )(sb_bias, page_table, qkv, qkv, qkv, tri, qkv_s, tri_page, cache_kt, cache_vt)


def _paged_sample(bias_ref, qkv, pages, tri, kt_sc, vt_sc):
    npg = len(pages)
    tq = qkv.shape[0]
    hd, w, ps = HEAD_DIM, W_GROUP, PAGE_SIZE
    nr = N_HEADS * tq
    for j, (k_ref, v_ref) in enumerate(pages):
        kt_sc[:, j * ps:(j + 1) * ps] = k_ref[...].reshape(w, ps).astype(BF16)
        vt_sc[:, j * ps:(j + 1) * ps] = v_ref[...].reshape(w, ps).astype(BF16)

    q, kn, vn = qkv[:, 0:w], qkv[:, w:2 * w], qkv[:, 2 * w:3 * w]
    lane_head = lax.broadcasted_iota(jnp.int32, (1, w), 1) // hd
    row = lax.broadcasted_iota(jnp.int32, (nr, 1), 0)
    row_head, row_t = row // tq, row % tq
    qbd = jnp.concatenate([jnp.where(lane_head == h, q, 0.0) for h in range(N_HEADS)], axis=0)
    bias = jnp.zeros((nr, 1), F32)
    for h in range(N_HEADS):
        bias = jnp.where(row_head == h, bias_ref[h], bias)

    run = jnp.zeros((nr, 1), F32)
    acc = jnp.zeros((nr, w), F32)
    for j in range(tq - 1, -1, -1):
        z = jnp.sum(qbd * kn[j:j + 1, :], axis=-1, keepdims=True) * SB_SCALE + bias
        seen = j < row_t
        run = run + jnp.where(seen, _softplus(z), 0.0)
        acc = acc + jnp.where(seen, jnp.exp(z - run), 0.0) * vn[j:j + 1, :]

    z_all = _dot(qbd.astype(BF16), kt_sc[...]) * SB_SCALE + bias
    z2 = jnp.concatenate([z_all[:, j * ps:(j + 1) * ps] for j in range(npg)], axis=0)
    in_page = _suffix_sum(_softplus(z2), tri)
    later = [None] * npg
    for j in range(npg - 1, -1, -1):
        later[j] = run
        run = run + in_page[j * nr:(j + 1) * nr, 0:1]
    p2 = jnp.exp(z2 - (in_page + jnp.concatenate(later, axis=0)))
    p_all = jnp.concatenate([p2[j * nr:(j + 1) * nr] for j in range(npg)], axis=1).astype(BF16)
    acc = acc + _dot_nt(p_all, vt_sc[...])
    out = jnp.where(lane_head == 0, acc[0:tq], 0.0)
    for h in range(1, N_HEADS):
        out = jnp.where(lane_head == h, acc[h * tq:(h + 1) * tq], out)
    return out


def _block_diag(wt):
    depth, nh, d, _ = wt.shape
    return jnp.einsum("lhij,hg->lhigj", wt, jnp.eye(nh, dtype=wt.dtype)).reshape(depth, nh * d, nh * d)


def _window_counts(p0, t):
    win = jnp.repeat(jnp.asarray(POOL_WINDOWS, jnp.int32), POOL_GROUP)
    pos = p0 + jnp.arange(t, dtype=jnp.int32)
    return jnp.minimum(win[None, :], pos[:, None] + 1).astype(F32).reshape(t, 1, W_GROUP)


def _to_time_major(a):
    return jnp.swapaxes(a, 0, 1)


def kernel(x_prompt, x_sample, cache_k, cache_v, state_rglru_h, state_conv_a, state_conv_b, state_pool, page_table,
           ln_in_g, ln_in_b, w_in, conv_a_w, conv_a_b, lru_wa, lru_ba, lru_wx, lru_bx, lru_lambda, conv_b_w,
           sb_bias, pool_w, pool_scale, w_out, ln1_g, ln1_b, w_ff1, w_ff2, ln2_g, ln2_b):
    depth = w_in.shape[0]
    alpha = (2.0 * depth) ** 0.25
    bp, tp, d = x_prompt.shape
    bs, ts, _ = x_sample.shape
    past_len = page_table.shape[1] * PAGE_SIZE
    w = W_GROUP

    tm_p = min(512, bp * tp)
    tm_s = min(512, bs * ts)
    bq = min(256, tp)
    tt_p = min(256, tp)
    cs_p = min(16, tt_p)

    cnt_p = _window_counts(0, tp)
    cnt_s = _window_counts(past_len, ts)
    tri_q = jnp.tril(jnp.ones((bq, bq), BF16), k=-1)
    tri_page = jnp.tril(jnp.ones((PAGE_SIZE, PAGE_SIZE), BF16))
    cache_kt = jnp.transpose(cache_k, (0, 1, 3, 4, 2))
    cache_vt = jnp.transpose(cache_v, (0, 1, 3, 4, 2))
    zeros = lambda *shape: jnp.zeros(shape, F32)

    nmix = (N_PROJ_GROUPS - 3) * w
    w_in_b = jnp.concatenate([w_in[..., :5 * w], w_in[..., 8 * w:], w_in[..., 5 * w:8 * w]], axis=-1).astype(BF16)
    stack_row = lambda a: a.reshape(depth, 1, -1)
    dense_w = (jnp.concatenate([w_out[:, :2 * w], w_out[:, 3 * w:]], axis=1).astype(BF16),
               w_out[:, 2 * w:3 * w].astype(BF16), stack_row(ln1_g), stack_row(ln1_b),
               w_ff1.astype(BF16), w_ff2.astype(BF16), stack_row(ln2_g), stack_row(ln2_b), w_in_b)

    xp, mix_p, qkv_p, k_stack, v_stack = _ln_inproj_call(x_prompt, ln_in_g, ln_in_b, w_in_b, nmix, bp, tp, tm_p, True)
    xs, mix_s, qkv_s = _ln_inproj_call(
        _to_time_major(x_sample).reshape(ts * bs, d), ln_in_g, ln_in_b, w_in_b, nmix, bs, ts, tm_s, False)

    mix_w = (conv_a_w, stack_row(conv_a_b), _block_diag(lru_wa).astype(BF16), stack_row(lru_ba),
             _block_diag(lru_wx).astype(BF16), stack_row(lru_bx), stack_row(lru_lambda), conv_b_w,
             _block_diag(pool_w).astype(BF16), stack_row(pool_scale))
    states_p = (zeros(1, bp, w), zeros(1, CONV_A - 1, bp, w), zeros(1, CONV_B - 1, bp, w), zeros(1, POOL_BUF, bp, w))
    states_s = (state_rglru_h, jnp.swapaxes(state_conv_a, 1, 2), jnp.swapaxes(state_conv_b, 1, 2),
                jnp.swapaxes(state_pool, 1, 2))

    outs = {name: [] for name in ("ks", "vs", "hp", "hs", "ap", "as", "bp", "bs", "pp", "ps")}
    for l in range(depth):
        yabd_p, h_n, a_n, b_n, p_n = _seqmix_call(mix_p, cnt_p, states_p, 0, mix_w, l, bp, tt_p, cs_p)
        outs["hp"].append(h_n)
        outs["ap"].append(a_n)
        outs["bp"].append(b_n)
        outs["pp"].append(p_n)
        yabd, h_n, a_n, b_n, p_n = _seqmix_call(mix_s, cnt_s, states_s, l, mix_w, l, bs, ts, 1)

        qkv = _to_time_major(qkv_s.reshape(ts, bs, 3 * w))
        yc_p, yc = _attn_call(qkv_p, qkv, cache_kt, cache_vt, page_table, sb_bias[l], tri_q, tri_page, l, bq)
        yc = _to_time_major(yc).reshape(ts * bs, w)

        res = _dense_call(yabd_p, yc_p, xp, dense_w, (k_stack, v_stack), l, nmix, bp, tp, tm_p, alpha, True)
        if l == depth - 1:
            y_prompt = res
        else:
            xp, mix_p, qkv_p, k_stack, v_stack = res
        res = _dense_call(yabd, yc, xs, dense_w, None, l, nmix, bs, ts, tm_s, alpha, False)
        if l == depth - 1:
            y_sample = _to_time_major(res.reshape(ts, bs, d))
        else:
            xs, mix_s, qkv_s = res
        outs["ks"].append(qkv[..., w:2 * w].reshape(bs, ts, N_HEADS, HEAD_DIM))
        outs["vs"].append(qkv[..., 2 * w:].reshape(bs, ts, N_HEADS, HEAD_DIM))
        outs["hs"].append(h_n)
        outs["as"].append(a_n)
        outs["bs"].append(b_n)
        outs["ps"].append(p_n)

    st = {name: jnp.stack(v) for name, v in outs.items()}
    heads = lambda a: a.reshape(depth, bp, tp, N_HEADS, HEAD_DIM)
    seq_first = lambda a: jnp.swapaxes(a, 1, 2)
    return (y_prompt, y_sample, heads(k_stack), heads(v_stack), st["ks"], st["vs"], st["hp"], st["hs"],
            seq_first(st["ap"]), seq_first(st["as"]), seq_first(st["bp"]), seq_first(st["bs"]),
            seq_first(st["pp"]), seq_first(st["ps"]))
```

```python
import functools

import jax
import jax.numpy as jnp
from jax import lax
from jax.experimental import pallas as pl
from jax.experimental.pallas import tpu as pltpu

F32 = jnp.float32
BF16 = jnp.bfloat16

N_HEADS = 4
W_GROUP = 256
HEAD_DIM = W_GROUP // N_HEADS
N_PROJ_GROUPS = 9
CONV_A = 4
CONV_B = 3
RGLRU_C = 8.0
POOL_WINDOWS = (2, 4, 8, 16)
POOL_BUF = max(POOL_WINDOWS) - 1
POOL_GROUP = W_GROUP // len(POOL_WINDOWS)
PAGE_SIZE = 128
LN_EPS = 1e-5
SB_SCALE = HEAD_DIM ** -0.5
LANES = 128
LOG2E = 1.4426950408889634

VMEM_LIMIT_BYTES = 56 * 1024 * 1024


def _cparams(*sem):
    return pltpu.CompilerParams(dimension_semantics=sem, vmem_limit_bytes=VMEM_LIMIT_BYTES)


def _layer_norm(x, g, b):
    mu = jnp.mean(x, axis=-1, keepdims=True)
    xc = x - mu
    var = jnp.mean(xc * xc, axis=-1, keepdims=True)
    return xc * lax.rsqrt(var + LN_EPS) * g + b


def _softplus(x):
    return jnp.maximum(x, 0.0) + jnp.log1p(jnp.exp(-jnp.abs(x)))


def _dot(a, b):
    return jnp.dot(a, b, preferred_element_type=F32)


def _dot_nt(a, b):
    return lax.dot_general(a, b, (((1,), (1,)), ((), ())), preferred_element_type=F32)


def _suffix_sum(x, tri):
    hi = x.astype(BF16)
    lo = (x - hi.astype(F32)).astype(BF16)
    return _dot(hi, tri) + _dot(lo, tri)


def _rows_from_blocks(src_ref, slab_ref, fn=lambda v: v):
    nb, tt, c = src_ref.shape
    for s in range(nb):
        v = fn(src_ref[s])
        for j in range(c // LANES):
            slab_ref[j, pl.ds(s, tt, stride=nb), :] = v[:, j * LANES:(j + 1) * LANES]
    return jnp.concatenate([slab_ref[j] for j in range(c // LANES)], axis=1)


def _blocks_from_rows(val, slab_ref, dst_ref, copies=()):
    nb, tt, c = dst_ref.shape
    for j in range(c // LANES):
        slab_ref[j] = val[:, j * LANES:(j + 1) * LANES]
    for s in range(nb):
        for j in range(c // LANES):
            piece = slab_ref[j, pl.ds(s, tt, stride=nb), :]
            dst_ref[s, :, j * LANES:(j + 1) * LANES] = piece
            for ref, c0 in copies:
                if c0 <= j * LANES < c0 + ref.shape[-1]:
                    ref[s, :, j * LANES - c0:(j + 1) * LANES - c0] = piece


def _emit_inproj(xb, w_ref, mix_ref, qkv_ref, kv_refs, slab_ref):
    ngroups, _, w = mix_ref.shape
    nmix = ngroups * w
    mix = _dot(xb, w_ref[:, :nmix])
    for grp in range(ngroups):
        mix_ref[grp] = mix[:, grp * w:(grp + 1) * w]
    if slab_ref is None:
        qkv_ref[...] = _dot(xb, w_ref[:, nmix:])
    else:
        k_ref, v_ref = kv_refs
        _blocks_from_rows(_dot(xb, w_ref[:, nmix:]), slab_ref, qkv_ref, ((k_ref, W_GROUP), (v_ref, 2 * W_GROUP)))


def _ln_inproj_kernel(x_ref, g_ref, b_ref, w_ref, *rest, batch_major):
    norm = lambda v: _layer_norm(v, g_ref[...], b_ref[...])
    if batch_major:
        xo_ref, mix_ref, qkv_ref, k_ref, v_ref, slab_x, slab_qkv = rest[2:]
        x0 = _rows_from_blocks(x_ref, slab_x, norm)
        xo_ref[...] = x0
        _emit_inproj(x0.astype(BF16), w_ref, mix_ref, qkv_ref, (k_ref, v_ref), slab_qkv)
    else:
        xo_ref, mix_ref, qkv_ref = rest
        x0 = norm(x_ref[...])
        xo_ref[...] = x0
        _emit_inproj(x0.astype(BF16), w_ref, mix_ref, qkv_ref, None, None)


def _dense_kernel(yabd_ref, yc_ref, x_ref, wabd_ref, wc_ref, g1_ref, b1_ref, w1_ref, w2_ref, g2_ref, b2_ref,
                  *rest, alpha, fc, batch_major, last):
    kv_refs = None
    if last:
        (y_ref, acc_ref), scratch = rest[:2], rest[2:]
    elif batch_major:
        win_ref, xo_ref, mix_ref, qkv_ref, k_ref, v_ref, acc_ref = rest[:1] + rest[3:9]
        kv_refs, scratch = (k_ref, v_ref), rest[9:]
    else:
        (win_ref, xo_ref, mix_ref, qkv_ref, acc_ref), scratch = rest[:5], rest[5:]
    yc = _rows_from_blocks(yc_ref, scratch[0]) if batch_major else yc_ref[...]
    mix = _dot(yabd_ref[...], wabd_ref[...]) + _dot(yc.astype(BF16), wc_ref[...])
    x1 = _layer_norm(alpha * x_ref[...] + mix, g1_ref[...], b1_ref[...])
    xb = x1.astype(BF16)
    for c in range(w1_ref.shape[1] // fc):
        h = jnp.maximum(_dot(xb, w1_ref[:, c * fc:(c + 1) * fc]), 0.0)
        part = _dot((h * h).astype(BF16), w2_ref[c * fc:(c + 1) * fc, :])
        if c == 0:
            acc_ref[...] = part
        else:
            acc_ref[...] += part
    x2 = _layer_norm(alpha * x1 + acc_ref[...], g2_ref[...], b2_ref[...])
    if not last:
        xo_ref[...] = x2
        _emit_inproj(x2.astype(BF16), win_ref, mix_ref, qkv_ref, kv_refs, scratch[1] if batch_major else None)
    elif batch_major:
        _blocks_from_rows(x2, scratch[1], y_ref)
    else:
        y_ref[...] = x2


def _resident(shape, index_map):
    return pl.BlockSpec(shape, index_map, pipeline_mode=pl.Buffered(1))


def _layer_block(a, layer):
    return _resident((None,) + a.shape[1:], lambda i: (layer, 0, 0))


def _slab(rows, c):
    return pltpu.VMEM((c // LANES, rows, LANES), F32)


def _row_specs(nb, t, tm, batch_major):
    rows = lambda c: (pl.BlockSpec((tm, c), lambda i: (i, 0)), (nb * t, c))
    if batch_major:
        blocks = lambda c: (pl.BlockSpec((nb, tm // nb, c), lambda i: (0, i, 0)), (nb, t, c))
    else:
        blocks = rows
    return rows, blocks


def _planes_spec(n, tm, c):
    return (pl.BlockSpec((c // W_GROUP, tm, W_GROUP), lambda i: (0, i, 0)), (c // W_GROUP, n, W_GROUP))


def _kv_stack_spec(depth, nb, t, tm, layer):
    return (pl.BlockSpec((None, nb, tm // nb, W_GROUP), lambda i: (layer, 0, i, 0)), (depth, nb, t, W_GROUP))


def _ln_inproj_call(x, g, b, w_in, nmix, nb, t, tm, batch_major):
    d = x.shape[-1]
    depth = w_in.shape[0]
    nqkv = w_in.shape[2] - nmix
    rows, blocks = _row_specs(nb, t, tm, batch_major)
    vec = _resident((1, d), lambda i: (0, 0))
    in_specs = [blocks(d)[0], vec, vec, _layer_block(w_in, 0)]
    operands = (x, g.reshape(1, d), b.reshape(1, d), w_in)
    out = [rows(d), _planes_spec(nb * t, tm, nmix), blocks(nqkv)]
    aliases = {}
    if batch_major:
        aliases = {len(operands): 3, len(operands) + 1: 4}
        in_specs += [pl.BlockSpec(memory_space=pl.ANY)] * 2
        operands += (jnp.zeros((depth, nb, t, W_GROUP), F32), jnp.zeros((depth, nb, t, W_GROUP), F32))
        out += [_kv_stack_spec(depth, nb, t, tm, 0)] * 2
    return pl.pallas_call(
        functools.partial(_ln_inproj_kernel, batch_major=batch_major), grid=(nb * t // tm,),
        in_specs=in_specs, out_specs=[s for s, _ in out],
        out_shape=[jax.ShapeDtypeStruct(shp, F32) for _, shp in out],
        scratch_shapes=[_slab(tm, d), _slab(tm, nqkv)] if batch_major else [],
        input_output_aliases=aliases, compiler_params=_cparams("arbitrary"), name="ln_in_proj")(*operands)


def _dense_call(yabd, yc, x, wts, kv_stacks, layer, nmix, nb, t, tm, alpha, batch_major, fc=512):
    d = x.shape[-1]
    w_in = wts[-1]
    depth = w_in.shape[0]
    last = layer == depth - 1
    nqkv = w_in.shape[2] - nmix
    rows, blocks = _row_specs(nb, t, tm, batch_major)
    in_specs = [rows(yabd.shape[-1])[0], blocks(yc.shape[-1])[0], rows(d)[0]] + [_layer_block(a, layer) for a in wts[:-1]]
    scratch = [pltpu.VMEM((tm, d), F32)]
    aliases = {}
    if batch_major:
        scratch += [_slab(tm, yc.shape[-1]), _slab(tm, d if last else nqkv)]
    if last:
        out = [blocks(d)]
        operands = (yabd, yc, x) + tuple(wts[:-1])
    else:
        in_specs.append(_layer_block(w_in, layer + 1))
        out = [rows(d), _planes_spec(nb * t, tm, nmix), blocks(nqkv)]
        operands = (yabd, yc, x) + tuple(wts)
        if batch_major:
            aliases = {len(operands): 3, len(operands) + 1: 4}
            in_specs += [pl.BlockSpec(memory_space=pl.ANY)] * 2
            out += [_kv_stack_spec(depth, nb, t, tm, layer + 1)] * 2
            operands += tuple(kv_stacks)
    res = pl.pallas_call(
        functools.partial(_dense_kernel, alpha=alpha, fc=fc, batch_major=batch_major, last=last),
        grid=(nb * t // tm,), in_specs=in_specs,
        out_specs=[s for s, _ in out], out_shape=[jax.ShapeDtypeStruct(shp, F32) for _, shp in out],
        scratch_shapes=scratch, input_output_aliases=aliases,
        compiler_params=_cparams("arbitrary"), name="dense_layer")(*operands)
    return res[0] if last else res


def _seqmix_kernel(xa_ref, ga_ref, xb_ref, gb_ref, gc_ref, xp_ref, cnt_ref,
                   h0_ref, bufa_ref, bufb_ref, bufp_ref,
                   caw_ref, cab_ref, wa_ref, ba_ref, wx_ref, bx_ref, lam_ref, cbw_ref, wp_ref, ps_ref,
                   y_ref, ho_ref, ao_ref, bo_ref, po_ref,
                   ea, eb, ep, h_sc, *, tt, nb, cs):
    w = W_GROUP
    i = pl.program_id(0)

    @pl.when(i == 0)
    def _():
        ea[0:CONV_A - 1] = bufa_ref[...]
        eb[0:CONV_B - 1] = bufb_ref[...]
        ep[0:POOL_BUF] = bufp_ref[...]
        h_sc[...] = h0_ref[...]

    @pl.when(i > 0)
    def _():
        ea[0:CONV_A - 1] = ea[tt:tt + CONV_A - 1]
        eb[0:CONV_B - 1] = eb[tt:tt + CONV_B - 1]
        ep[0:POOL_BUF] = ep[tt:tt + POOL_BUF]

    ea[CONV_A - 1:CONV_A - 1 + tt] = xa_ref[...].reshape(tt, nb, w)
    ep[POOL_BUF:POOL_BUF + tt] = xp_ref[...].reshape(tt, nb, w)

    coef = -RGLRU_C * _softplus(-lam_ref[...])
    group = lax.broadcasted_iota(jnp.int32, (1, 1, w), 2) // POOL_GROUP
    rows = cs * nb

    def chunk(k, carry):
        t0 = pl.multiple_of(k * cs, cs)
        r0 = pl.multiple_of(k * rows, rows)

        xc = caw_ref[0:1, :][None] * ea[pl.ds(t0, cs)]
        for j in range(1, CONV_A):
            xc = xc + caw_ref[j:j + 1, :][None] * ea[pl.ds(t0 + j, cs)]
        xc = (xc + cab_ref[...][None]).reshape(rows, w)
        xcb = xc.astype(BF16)
        r = jax.nn.sigmoid(_dot(xcb, wa_ref[...]) + ba_ref[...])
        gi = jax.nn.sigmoid(_dot(xcb, wx_ref[...]) + bx_ref[...])
        log_a = coef * r
        a = jnp.exp(log_a)
        a3 = a.reshape(cs, nb, w)
        b3 = (jnp.sqrt(jnp.tanh(-log_a) * (a * a + 1.0)) * (gi * xc)).reshape(cs, nb, w)
        h = h_sc[...]
        hs = []
        for t in range(cs):
            h = a3[t] * h + b3[t]
            hs.append(h[None])
        h_sc[...] = h
        hseq = jnp.concatenate(hs, axis=0).reshape(rows, w)
        y_ref[pl.ds(r0, rows), 0:w] = (hseq * jax.nn.gelu(ga_ref[pl.ds(r0, rows), :])).astype(y_ref.dtype)

        u = gc_ref[pl.ds(r0, rows), :] * xb_ref[pl.ds(r0, rows), :]
        eb[pl.ds(t0 + CONV_B - 1, cs)] = u.reshape(cs, nb, w)
        cb = cbw_ref[0:1, :][None] * eb[pl.ds(t0, cs)]
        for j in range(1, CONV_B):
            cb = cb + cbw_ref[j:j + 1, :][None] * eb[pl.ds(t0 + j, cs)]
        y_ref[pl.ds(r0, rows), w:2 * w] = (gb_ref[pl.ds(r0, rows), :] * cb.reshape(rows, w)).astype(y_ref.dtype)

        e = ep[pl.ds(t0, cs + POOL_BUF)]
        s2 = e[1:] + e[:-1]
        s4 = s2[2:] + s2[:-2]
        s8 = s4[4:] + s4[:-4]
        s16 = s8[8:] + s8[:-8]
        win = jnp.where(group == 0, s2[14:], jnp.where(group == 1, s4[12:], jnp.where(group == 2, s8[8:], s16)))
        d = (win / cnt_ref[pl.ds(t0, cs)] - e[POOL_BUF:]).reshape(rows, w)
        y_ref[pl.ds(r0, rows), 2 * w:3 * w] = (_dot(d.astype(BF16), wp_ref[...]) * ps_ref[...]).astype(y_ref.dtype)
        return carry

    lax.fori_loop(0, tt // cs, chunk, 0)

    ho_ref[...] = h_sc[...]
    ao_ref[...] = ea[tt:tt + CONV_A - 1]
    bo_ref[...] = eb[tt:tt + CONV_B - 1]
    po_ref[...] = ep[tt:tt + POOL_BUF]


def _seqmix_call(mix, cnt, states, state_layer, wts, layer, nb, tt, cs):
    n = mix.shape[1]
    t_total = n // nb
    w = W_GROUP
    assert t_total % tt == 0 and tt % cs == 0 and (tt >= POOL_BUF or tt == t_total)
    plane = lambda c: pl.BlockSpec((None, tt * nb, w), lambda i, c=c: (c, i, 0))
    entry = lambda a, j: pl.BlockSpec((None,) + a.shape[1:], lambda i: (j,) + (0,) * (a.ndim - 1))
    whole = lambda a: pl.BlockSpec(a.shape, lambda i: (0,) * len(a.shape))
    state_shapes = [jax.ShapeDtypeStruct(a.shape[1:], F32) for a in states]
    return pl.pallas_call(
        functools.partial(_seqmix_kernel, tt=tt, nb=nb, cs=cs), grid=(t_total // tt,),
        in_specs=[plane(c) for c in range(mix.shape[0])] + [pl.BlockSpec((tt, 1, w), lambda i: (i, 0, 0))]
        + [entry(a, state_layer) for a in states] + [entry(a, layer) for a in wts],
        out_specs=[pl.BlockSpec((tt * nb, 3 * w), lambda i: (i, 0))] + [whole(a) for a in state_shapes],
        out_shape=[jax.ShapeDtypeStruct((n, 3 * w), BF16)] + state_shapes,
        scratch_shapes=[pltpu.VMEM((tt + CONV_A - 1, nb, w), F32), pltpu.VMEM((tt + CONV_B - 1, nb, w), F32),
                        pltpu.VMEM((tt + POOL_BUF, nb, w), F32), pltpu.VMEM((nb, w), F32)],
        compiler_params=_cparams("arbitrary"), name="seq_mixers",
    )(*([mix] * mix.shape[0]), cnt, *states, *wts)


def _attn_kernel(bias_ref, pt_ref, q_ref, k_ref, v_ref, tri_ref, qkvs_ref, trip_ref, ckt_ref, cvt_ref,
                 o_ref, os_ref, kb, vb, acc_sc, c_sc, kpage, vpage, kt_sc, vt_sc, sem, *, bq, layer, nact):
    i = pl.program_id(1)
    hd = HEAD_DIM
    t = k_ref.shape[0]
    heads = range(N_HEADS)
    pad = kb.shape[2] - hd - 2

    step = pl.program_id(0) * pl.num_programs(1) + i
    nsamp, tq, _ = qkvs_ref.shape
    npg = kpage.shape[1] // nsamp

    def page_copies(st, slot):
        copies = []
        for k in range(nsamp):
            for j in range(npg):
                page = pt_ref[st * nsamp + k, j]
                copies.append(pltpu.make_async_copy(ckt_ref.at[layer, page], kpage.at[slot, k * npg + j], sem.at[slot, 0]))
                copies.append(pltpu.make_async_copy(cvt_ref.at[layer, page], vpage.at[slot, k * npg + j], sem.at[slot, 1]))
        return copies

    @pl.when(step == 0)
    def _():
        for cp in page_copies(0, 0):
            cp.start()

    @pl.when(step + 1 < nact)
    def _():
        for cp in page_copies(step + 1, (step + 1) % 2):
            cp.start()

    @pl.when(step < nact)
    def _():
        slot = step % 2
        for cp in page_copies(step, slot):
            cp.wait()
        for k in range(nsamp):
            pages = [(kpage.at[slot, k * npg + j], vpage.at[slot, k * npg + j]) for j in range(npg)]
            os_ref[k] = _paged_sample(bias_ref, qkvs_ref[k], pages, trip_ref[...], kt_sc, vt_sc)

    @pl.when(i == 0)
    def _():
        for h in heads:
            b = jnp.full((t, 1), bias_ref[h], F32)
            b_hi = b.astype(BF16)
            b_lo = (b - b_hi.astype(F32)).astype(BF16)
            kb[h] = jnp.concatenate(
                [k_ref[:, h * hd:(h + 1) * hd].astype(BF16), b_hi, b_lo, jnp.zeros((t, pad), BF16)], axis=1)
            vb[h] = v_ref[:, h * hd:(h + 1) * hd].astype(BF16)

    tri = tri_ref[...]
    q_tail = jnp.concatenate([jnp.ones((bq, 2), BF16), jnp.zeros((bq, pad), BF16)], axis=1)
    qs = [jnp.concatenate([(q_ref[:, h * hd:(h + 1) * hd] * SB_SCALE).astype(BF16), q_tail], axis=1) for h in heads]
    causal = lax.broadcasted_iota(jnp.int32, (bq, bq), 1) < lax.broadcasted_iota(jnp.int32, (bq, bq), 0)

    def sweep(blocks, first):
        rows = [pl.ds(pl.multiple_of(j * bq, bq), bq) for j, _ in blocks]
        units = [(n, h) for n in range(len(blocks)) for h in heads]
        carry = [None if first else c_sc[h] for h in heads]
        pvs = [None] * N_HEADS
        z, ls, col0, later = {}, {}, {}, {}
        for k in range(len(units) + 2):
            if k < len(units):
                n, h = units[k]
                z[k] = _dot_nt(qs[h], kb[h, rows[n], :])
            if 1 <= k <= len(units):
                mask = blocks[units[k - 1][0]][1]
                zz = z.pop(k - 1)
                sp = jnp.maximum(zz, 0.0) + jnp.log(1.0 + jnp.exp2(jnp.abs(zz) * -LOG2E))
                ls[k - 1] = zz - sp
                if mask is not None:
                    sp = jnp.where(mask, sp, 0.0)
                col0[k - 1] = sp[:, 0:1]
                later[k - 1] = _dot(sp.astype(BF16), tri)
            if 2 <= k:
                n, h = units[k - 2]
                mask = blocks[n][1]
                lt = later.pop(k - 2)
                e = ls.pop(k - 2) - lt
                if carry[h] is not None:
                    e = e - carry[h]
                p = jnp.exp(e)
                if mask is not None:
                    p = jnp.where(mask, p, 0.0)
                total = lt[:, 0:1] + col0.pop(k - 2)
                carry[h] = total if carry[h] is None else carry[h] + total
                pv = _dot(p.astype(BF16), vb[h, rows[n], :])
                pvs[h] = pv if pvs[h] is None else pvs[h] + pv
        for h in heads:
            c_sc[h] = carry[h]
            acc_sc[h] = pvs[h] if first else acc_sc[h] + pvs[h]

    @pl.when(i % 2 == 0)
    def _():
        sweep([(i, causal)], True)

    @pl.when(i % 2 == 1)
    def _():
        sweep([(i, causal), (i - 1, None)], True)

    top = i - 1 - i % 2

    def body(pair, carry):
        j = top - 2 * pair
        sweep([(j, None), (j - 1, None)], False)
        return carry

    lax.fori_loop(0, i // 2, body, 0)
    o_ref[...] = jnp.concatenate([acc_sc[h] for h in heads], axis=-1)


def _attn_call(qkv, qkv_s, cache_kt, cache_vt, page_table, sb_bias, tri, tri_page, layer, bq):
    nb, t, _ = qkv.shape
    ns, tq, _ = qkv_s.shape
    npg = page_table.shape[1]
    w = W_GROUP
    nsteps = nb * (t // bq)
    nsamp = -(-ns // nsteps)
    nact = ns // nsamp
    assert ns % nsamp == 0 and (N_HEADS * tq) % 8 == 0
    sample_block = lambda b, i, s, p: (jnp.minimum(b * (t // bq) + i, nact - 1), 0, 0)
    page_ring = pltpu.VMEM((2, nsamp * npg, N_HEADS, HEAD_DIM, PAGE_SIZE), F32)
    grid_spec = pltpu.PrefetchScalarGridSpec(
        num_scalar_prefetch=2, grid=(nb, t // bq),
        in_specs=[pl.BlockSpec((None, bq, w), lambda b, i, s, p: (b, i, 0)),
                  pl.BlockSpec((None, t, w), lambda b, i, s, p: (b, 0, 1)),
                  pl.BlockSpec((None, t, w), lambda b, i, s, p: (b, 0, 2)),
                  pl.BlockSpec((bq, bq), lambda b, i, s, p: (0, 0)),
                  pl.BlockSpec((nsamp, tq, 3 * w), sample_block),
                  pl.BlockSpec((PAGE_SIZE, PAGE_SIZE), lambda b, i, s, p: (0, 0)),
                  pl.BlockSpec(memory_space=pl.ANY), pl.BlockSpec(memory_space=pl.ANY)],
        out_specs=[pl.BlockSpec((None, bq, w), lambda b, i, s, p: (b, i, 0)),
                   pl.BlockSpec((nsamp, tq, w), sample_block)],
        scratch_shapes=[pltpu.VMEM((N_HEADS, t, LANES), BF16), pltpu.VMEM((N_HEADS, t, HEAD_DIM), BF16),
                        pltpu.VMEM((N_HEADS, bq, HEAD_DIM), F32), pltpu.VMEM((N_HEADS, bq, 1), F32),
                        page_ring, page_ring,
                        pltpu.VMEM((w, npg * PAGE_SIZE), BF16), pltpu.VMEM((w, npg * PAGE_SIZE), BF16),
                        pltpu.SemaphoreType.DMA((2, 2))])
    return pl.pallas_call(
        functools.partial(_attn_kernel, bq=bq, layer=layer, nact=nact), grid_spec=grid_spec,
        out_shape=[jax.ShapeDtypeStruct((nb, t, w), F32), jax.ShapeDtypeStruct((ns, tq, w), F32)],
        compiler_params=_cparams("arbitrary", "arbitrary"), name="sb_attention",
    )(sb_bias, page_table, qkv, qkv, qkv, tri, qkv_s, tri_page, cache_kt, cache_vt)


def _paged_sample(bias_ref, qkv, pages, tri, kt_sc, vt_sc):
    npg = len(pages)
    tq = qkv.shape[0]
    hd, w, ps = HEAD_DIM, W_GROUP, PAGE_SIZE
    nr = N_HEADS * tq
    for j, (k_ref, v_ref) in enumerate(pages):
        kt_sc[:, j * ps:(j + 1) * ps] = k_ref[...].reshape(w, ps).astype(BF16)
        vt_sc[:, j * ps:(j + 1) * ps] = v_ref[...].reshape(w, ps).astype(BF16)

    q, kn, vn = qkv[:, 0:w], qkv[:, w:2 * w], qkv[:, 2 * w:3 * w]
    lane_head = lax.broadcasted_iota(jnp.int32, (1, w), 1) // hd
    row = lax.broadcasted_iota(jnp.int32, (nr, 1), 0)
    row_head, row_t = row // tq, row % tq
    qbd = jnp.concatenate([jnp.where(lane_head == h, q, 0.0) for h in range(N_HEADS)], axis=0)
    bias = jnp.zeros((nr, 1), F32)
    for h in range(N_HEADS):
        bias = jnp.where(row_head == h, bias_ref[h], bias)

    run = jnp.zeros((nr, 1), F32)
    acc = jnp.zeros((nr, w), F32)
    for j in range(tq - 1, -1, -1):
        z = jnp.sum(qbd * kn[j:j + 1, :], axis=-1, keepdims=True) * SB_SCALE + bias
        seen = j < row_t
        run = run + jnp.where(seen, _softplus(z), 0.0)
        acc = acc + jnp.where(seen, jnp.exp(z - run), 0.0) * vn[j:j + 1, :]

    z_all = _dot(qbd.astype(BF16), kt_sc[...]) * SB_SCALE + bias
    z2 = jnp.concatenate([z_all[:, j * ps:(j + 1) * ps] for j in range(npg)], axis=0)
    in_page = _suffix_sum(_softplus(z2), tri)
    later = [None] * npg
    for j in range(npg - 1, -1, -1):
        later[j] = run
        run = run + in_page[j * nr:(j + 1) * nr, 0:1]
    p2 = jnp.exp(z2 - (in_page + jnp.concatenate(later, axis=0)))
    p_all = jnp.concatenate([p2[j * nr:(j + 1) * nr] for j in range(npg)], axis=1).astype(BF16)
    acc = acc + _dot_nt(p_all, vt_sc[...])
    out = jnp.where(lane_head == 0, acc[0:tq], 0.0)
    for h in range(1, N_HEADS):
        out = jnp.where(lane_head == h, acc[h * tq:(h + 1) * tq], out)
    return out


def _block_diag(wt):
    depth, nh, d, _ = wt.shape
    return jnp.einsum("lhij,hg->lhigj", wt, jnp.eye(nh, dtype=wt.dtype)).reshape(depth, nh * d, nh * d)


def _window_counts(p0, t):
    win = jnp.repeat(jnp.asarray(POOL_WINDOWS, jnp.int32), POOL_GROUP)
    pos = p0 + jnp.arange(t, dtype=jnp.int32)
    return jnp.minimum(win[None, :], pos[:, None] + 1).astype(F32).reshape(t, 1, W_GROUP)


def _to_time_major(a):
    return jnp.swapaxes(a, 0, 1)


def kernel(x_prompt, x_sample, cache_k, cache_v, state_rglru_h, state_conv_a, state_conv_b, state_pool, page_table,
           ln_in_g, ln_in_b, w_in, conv_a_w, conv_a_b, lru_wa, lru_ba, lru_wx, lru_bx, lru_lambda, conv_b_w,
           sb_bias, pool_w, pool_scale, w_out, ln1_g, ln1_b, w_ff1, w_ff2, ln2_g, ln2_b):
    depth = w_in.shape[0]
    alpha = (2.0 * depth) ** 0.25
    bp, tp, d = x_prompt.shape
    bs, ts, _ = x_sample.shape
    past_len = page_table.shape[1] * PAGE_SIZE
    w = W_GROUP

    tm_p = min(512, bp * tp)
    tm_s = min(512, bs * ts)
    bq = min(256, tp)
    tt_p = min(256, tp)
    cs_p = min(16, tt_p)

    cnt_p = _window_counts(0, tp)
    cnt_s = _window_counts(past_len, ts)
    tri_q = jnp.tril(jnp.ones((bq, bq), BF16), k=-1)
    tri_page = jnp.tril(jnp.ones((PAGE_SIZE, PAGE_SIZE), BF16))
    cache_kt = jnp.transpose(cache_k, (0, 1, 3, 4, 2))
    cache_vt = jnp.transpose(cache_v, (0, 1, 3, 4, 2))
    zeros = lambda *shape: jnp.zeros(shape, F32)

    nmix = (N_PROJ_GROUPS - 3) * w
    w_in_b = jnp.concatenate([w_in[..., :5 * w], w_in[..., 8 * w:], w_in[..., 5 * w:8 * w]], axis=-1).astype(BF16)
    stack_row = lambda a: a.reshape(depth, 1, -1)
    dense_w = (jnp.concatenate([w_out[:, :2 * w], w_out[:, 3 * w:]], axis=1).astype(BF16),
               w_out[:, 2 * w:3 * w].astype(BF16), stack_row(ln1_g), stack_row(ln1_b),
               w_ff1.astype(BF16), w_ff2.astype(BF16), stack_row(ln2_g), stack_row(ln2_b), w_in_b)

    xp, mix_p, qkv_p, k_stack, v_stack = _ln_inproj_call(x_prompt, ln_in_g, ln_in_b, w_in_b, nmix, bp, tp, tm_p, True)
    xs, mix_s, qkv_s = _ln_inproj_call(
        _to_time_major(x_sample).reshape(ts * bs, d), ln_in_g, ln_in_b, w_in_b, nmix, bs, ts, tm_s, False)

    mix_w = (conv_a_w, stack_row(conv_a_b), _block_diag(lru_wa).astype(BF16), stack_row(lru_ba),
             _block_diag(lru_wx).astype(BF16), stack_row(lru_bx), stack_row(lru_lambda), conv_b_w,
             _block_diag(pool_w).astype(BF16), stack_row(pool_scale))
    states_p = (zeros(1, bp, w), zeros(1, CONV_A - 1, bp, w), zeros(1, CONV_B - 1, bp, w), zeros(1, POOL_BUF, bp, w))
    states_s = (state_rglru_h, jnp.swapaxes(state_conv_a, 1, 2), jnp.swapaxes(state_conv_b, 1, 2),
                jnp.swapaxes(state_pool, 1, 2))

    outs = {name: [] for name in ("ks", "vs", "hp", "hs", "ap", "as", "bp", "bs", "pp", "ps")}
    for l in range(depth):
        yabd_p, h_n, a_n, b_n, p_n = _seqmix_call(mix_p, cnt_p, states_p, 0, mix_w, l, bp, tt_p, cs_p)
        outs["hp"].append(h_n)
        outs["ap"].append(a_n)
        outs["bp"].append(b_n)
        outs["pp"].append(p_n)
        yabd, h_n, a_n, b_n, p_n = _seqmix_call(mix_s, cnt_s, states_s, l, mix_w, l, bs, ts, 1)

        qkv = _to_time_major(qkv_s.reshape(ts, bs, 3 * w))
        yc_p, yc = _attn_call(qkv_p, qkv, cache_kt, cache_vt, page_table, sb_bias[l], tri_q, tri_page, l, bq)
        yc = _to_time_major(yc).reshape(ts * bs, w)

        res = _dense_call(yabd_p, yc_p, xp, dense_w, (k_stack, v_stack), l, nmix, bp, tp, tm_p, alpha, True)
        if l == depth - 1:
            y_prompt = res
        else:
            xp, mix_p, qkv_p, k_stack, v_stack = res
        res = _dense_call(yabd, yc, xs, dense_w, None, l, nmix, bs, ts, tm_s, alpha, False)
        if l == depth - 1:
            y_sample = _to_time_major(res.reshape(ts, bs, d))
        else:
            xs, mix_s, qkv_s = res
        outs["ks"].append(qkv[..., w:2 * w].reshape(bs, ts, N_HEADS, HEAD_DIM))
        outs["vs"].append(qkv[..., 2 * w:].reshape(bs, ts, N_HEADS, HEAD_DIM))
        outs["hs"].append(h_n)
        outs["as"].append(a_n)
        outs["bs"].append(b_n)
        outs["ps"].append(p_n)

    st = {name: jnp.stack(v) for name, v in outs.items()}
    heads = lambda a: a.reshape(depth, bp, tp, N_HEADS, HEAD_DIM)
    seq_first = lambda a: jnp.swapaxes(a, 1, 2)
    return (y_prompt, y_sample, heads(k_stack), heads(v_stack), st["ks"], st["vs"], st["hp"], st["hs"],
            seq_first(st["ap"]), seq_first(st["as"]), seq_first(st["bp"]), seq_first(st["bs"]),
            seq_first(st["pp"]), seq_first(st["ps"]))
```

```python
import functools

import jax
import jax.numpy as jnp
from jax import lax
from jax.experimental import pallas as pl
from jax.experimental.pallas import tpu as pltpu

F32 = jnp.float32
BF16 = jnp.bfloat16

N_HEADS = 4
W_GROUP = 256
HEAD_DIM = W_GROUP // N_HEADS
N_PROJ_GROUPS = 9
CONV_A = 4
CONV_B = 3
RGLRU_C = 8.0
POOL_WINDOWS = (2, 4, 8, 16)
POOL_BUF = max(POOL_WINDOWS) - 1
POOL_GROUP = W_GROUP // len(POOL_WINDOWS)
PAGE_SIZE = 128
LN_EPS = 1e-5
SB_SCALE = HEAD_DIM ** -0.5
LANES = 128
LOG2E = 1.4426950408889634

VMEM_LIMIT_BYTES = 56 * 1024 * 1024


def _cparams(*sem):
    return pltpu.CompilerParams(dimension_semantics=sem, vmem_limit_bytes=VMEM_LIMIT_BYTES)


def _layer_norm(x, g, b):
    mu = jnp.mean(x, axis=-1, keepdims=True)
    xc = x - mu
    var = jnp.mean(xc * xc, axis=-1, keepdims=True)
    return xc * lax.rsqrt(var + LN_EPS) * g + b


def _softplus(x):
    return jnp.maximum(x, 0.0) + jnp.log1p(jnp.exp(-jnp.abs(x)))


def _dot(a, b):
    return jnp.dot(a, b, preferred_element_type=F32)


def _dot_nt(a, b):
    return lax.dot_general(a, b, (((1,), (1,)), ((), ())), preferred_element_type=F32)


def _suffix_sum(x, tri):
    hi = x.astype(BF16)
    lo = (x - hi.astype(F32)).astype(BF16)
    return _dot(hi, tri) + _dot(lo, tri)


def _rows_from_blocks(src_ref, slab_ref, fn=lambda v: v):
    nb, tt, c = src_ref.shape
    for s in range(nb):
        v = fn(src_ref[s])
        for j in range(c // LANES):
            slab_ref[j, pl.ds(s, tt, stride=nb), :] = v[:, j * LANES:(j + 1) * LANES]
    return jnp.concatenate([slab_ref[j] for j in range(c // LANES)], axis=1)


def _blocks_from_rows(val, slab_ref, dst_ref, copies=()):
    nb, tt, c = dst_ref.shape
    for j in range(c // LANES):
        slab_ref[j] = val[:, j * LANES:(j + 1) * LANES]
    for s in range(nb):
        for j in range(c // LANES):
            piece = slab_ref[j, pl.ds(s, tt, stride=nb), :]
            dst_ref[s, :, j * LANES:(j + 1) * LANES] = piece
            for ref, c0 in copies:
                if c0 <= j * LANES < c0 + ref.shape[-1]:
                    ref[s, :, j * LANES - c0:(j + 1) * LANES - c0] = piece


def _emit_inproj(xb, w_ref, mix_ref, qkv_ref, kv_refs, slab_ref):
    ngroups, _, w = mix_ref.shape
    nmix = ngroups * w
    mix = _dot(xb, w_ref[:, :nmix])
    for grp in range(ngroups):
        mix_ref[grp] = mix[:, grp * w:(grp + 1) * w]
    if slab_ref is None:
        qkv_ref[...] = _dot(xb, w_ref[:, nmix:])
    else:
        k_ref, v_ref = kv_refs
        _blocks_from_rows(_dot(xb, w_ref[:, nmix:]), slab_ref, qkv_ref, ((k_ref, W_GROUP), (v_ref, 2 * W_GROUP)))


def _ln_inproj_kernel(x_ref, g_ref, b_ref, w_ref, *rest, batch_major):
    norm = lambda v: _layer_norm(v, g_ref[...], b_ref[...])
    if batch_major:
        xo_ref, mix_ref, qkv_ref, k_ref, v_ref, slab_x, slab_qkv = rest[2:]
        x0 = _rows_from_blocks(x_ref, slab_x, norm)
        xo_ref[...] = x0
        _emit_inproj(x0.astype(BF16), w_ref, mix_ref, qkv_ref, (k_ref, v_ref), slab_qkv)
    else:
        xo_ref, mix_ref, qkv_ref = rest
        x0 = norm(x_ref[...])
        xo_ref[...] = x0
        _emit_inproj(x0.astype(BF16), w_ref, mix_ref, qkv_ref, None, None)


def _dense_kernel(yabd_ref, yc_ref, x_ref, wabd_ref, wc_ref, g1_ref, b1_ref, w1_ref, w2_ref, g2_ref, b2_ref,
                  *rest, alpha, fc, batch_major, last):
    kv_refs = None
    if last:
        (y_ref, acc_ref), scratch = rest[:2], rest[2:]
    elif batch_major:
        win_ref, xo_ref, mix_ref, qkv_ref, k_ref, v_ref, acc_ref = rest[:1] + rest[3:9]
        kv_refs, scratch = (k_ref, v_ref), rest[9:]
    else:
        (win_ref, xo_ref, mix_ref, qkv_ref, acc_ref), scratch = rest[:5], rest[5:]
    yc = _rows_from_blocks(yc_ref, scratch[0]) if batch_major else yc_ref[...]
    mix = _dot(yabd_ref[...], wabd_ref[...]) + _dot(yc.astype(BF16), wc_ref[...])
    x1 = _layer_norm(alpha * x_ref[...] + mix, g1_ref[...], b1_ref[...])
    xb = x1.astype(BF16)
    for c in range(w1_ref.shape[1] // fc):
        h = jnp.maximum(_dot(xb, w1_ref[:, c * fc:(c + 1) * fc]), 0.0)
        part = _dot((h * h).astype(BF16), w2_ref[c * fc:(c + 1) * fc, :])
        if c == 0:
            acc_ref[...] = part
        else:
            acc_ref[...] += part
    x2 = _layer_norm(alpha * x1 + acc_ref[...], g2_ref[...], b2_ref[...])
    if not last:
        xo_ref[...] = x2
        _emit_inproj(x2.astype(BF16), win_ref, mix_ref, qkv_ref, kv_refs, scratch[1] if batch_major else None)
    elif batch_major:
        _blocks_from_rows(x2, scratch[1], y_ref)
    else:
        y_ref[...] = x2


def _resident(shape, index_map):
    return pl.BlockSpec(shape, index_map, pipeline_mode=pl.Buffered(1))


def _layer_block(a, layer):
    return _resident((None,) + a.shape[1:], lambda i: (layer, 0, 0))


def _slab(rows, c):
    return pltpu.VMEM((c // LANES, rows, LANES), F32)


def _row_specs(nb, t, tm, batch_major):
    rows = lambda c: (pl.BlockSpec((tm, c), lambda i: (i, 0)), (nb * t, c))
    if batch_major:
        blocks = lambda c: (pl.BlockSpec((nb, tm // nb, c), lambda i: (0, i, 0)), (nb, t, c))
    else:
        blocks = rows
    return rows, blocks


def _planes_spec(n, tm, c):
    return (pl.BlockSpec((c // W_GROUP, tm, W_GROUP), lambda i: (0, i, 0)), (c // W_GROUP, n, W_GROUP))


def _kv_stack_spec(depth, nb, t, tm, layer):
    return (pl.BlockSpec((None, nb, tm // nb, W_GROUP), lambda i: (layer, 0, i, 0)), (depth, nb, t, W_GROUP))


def _ln_inproj_call(x, g, b, w_in, nmix, nb, t, tm, batch_major):
    d = x.shape[-1]
    depth = w_in.shape[0]
    nqkv = w_in.shape[2] - nmix
    rows, blocks = _row_specs(nb, t, tm, batch_major)
    vec = _resident((1, d), lambda i: (0, 0))
    in_specs = [blocks(d)[0], vec, vec, _layer_block(w_in, 0)]
    operands = (x, g.reshape(1, d), b.reshape(1, d), w_in)
    out = [rows(d), _planes_spec(nb * t, tm, nmix), blocks(nqkv)]
    aliases = {}
    if batch_major:
        aliases = {len(operands): 3, len(operands) + 1: 4}
        in_specs += [pl.BlockSpec(memory_space=pl.ANY)] * 2
        operands += (jnp.zeros((depth, nb, t, W_GROUP), F32), jnp.zeros((depth, nb, t, W_GROUP), F32))
        out += [_kv_stack_spec(depth, nb, t, tm, 0)] * 2
    return pl.pallas_call(
        functools.partial(_ln_inproj_kernel, batch_major=batch_major), grid=(nb * t // tm,),
        in_specs=in_specs, out_specs=[s for s, _ in out],
        out_shape=[jax.ShapeDtypeStruct(shp, F32) for _, shp in out],
        scratch_shapes=[_slab(tm, d), _slab(tm, nqkv)] if batch_major else [],
        input_output_aliases=aliases, compiler_params=_cparams("arbitrary"), name="ln_in_proj")(*operands)


def _dense_call(yabd, yc, x, wts, kv_stacks, layer, nmix, nb, t, tm, alpha, batch_major, fc=512):
    d = x.shape[-1]
    w_in = wts[-1]
    depth = w_in.shape[0]
    last = layer == depth - 1
    nqkv = w_in.shape[2] - nmix
    rows, blocks = _row_specs(nb, t, tm, batch_major)
    in_specs = [rows(yabd.shape[-1])[0], blocks(yc.shape[-1])[0], rows(d)[0]] + [_layer_block(a, layer) for a in wts[:-1]]
    scratch = [pltpu.VMEM((tm, d), F32)]
    aliases = {}
    if batch_major:
        scratch += [_slab(tm, yc.shape[-1]), _slab(tm, d if last else nqkv)]
    if last:
        out = [blocks(d)]
        operands = (yabd, yc, x) + tuple(wts[:-1])
    else:
        in_specs.append(_layer_block(w_in, layer + 1))
        out = [rows(d), _planes_spec(nb * t, tm, nmix), blocks(nqkv)]
        operands = (yabd, yc, x) + tuple(wts)
        if batch_major:
            aliases = {len(operands): 3, len(operands) + 1: 4}
            in_specs += [pl.BlockSpec(memory_space=pl.ANY)] * 2
            out += [_kv_stack_spec(depth, nb, t, tm, layer + 1)] * 2
            operands += tuple(kv_stacks)
    res = pl.pallas_call(
        functools.partial(_dense_kernel, alpha=alpha, fc=fc, batch_major=batch_major, last=last),
        grid=(nb * t // tm,), in_specs=in_specs,
        out_specs=[s for s, _ in out], out_shape=[jax.ShapeDtypeStruct(shp, F32) for _, shp in out],
        scratch_shapes=scratch, input_output_aliases=aliases,
        compiler_params=_cparams("arbitrary"), name="dense_layer")(*operands)
    return res[0] if last else res


def _seqmix_kernel(xa_ref, ga_ref, xb_ref, gb_ref, gc_ref, xp_ref, cnt_ref,
                   h0_ref, bufa_ref, bufb_ref, bufp_ref,
                   caw_ref, cab_ref, wa_ref, ba_ref, wx_ref, bx_ref, lam_ref, cbw_ref, wp_ref, ps_ref,
                   y_ref, ho_ref, ao_ref, bo_ref, po_ref,
                   ea, eb, ep, h_sc, *, tt, nb, cs):
    w = W_GROUP
    i = pl.program_id(0)

    @pl.when(i == 0)
    def _():
        ea[0:CONV_A - 1] = bufa_ref[...]
        eb[0:CONV_B - 1] = bufb_ref[...]
        ep[0:POOL_BUF] = bufp_ref[...]
        h_sc[...] = h0_ref[...]

    @pl.when(i > 0)
    def _():
        ea[0:CONV_A - 1] = ea[tt:tt + CONV_A - 1]
        eb[0:CONV_B - 1] = eb[tt:tt + CONV_B - 1]
        ep[0:POOL_BUF] = ep[tt:tt + POOL_BUF]

    ea[CONV_A - 1:CONV_A - 1 + tt] = xa_ref[...].reshape(tt, nb, w)
    ep[POOL_BUF:POOL_BUF + tt] = xp_ref[...].reshape(tt, nb, w)

    coef = -RGLRU_C * _softplus(-lam_ref[...])
    group = lax.broadcasted_iota(jnp.int32, (1, 1, w), 2) // POOL_GROUP
    rows = cs * nb

    def chunk(k, carry):
        t0 = pl.multiple_of(k * cs, cs)
        r0 = pl.multiple_of(k * rows, rows)

        xc = caw_ref[0:1, :][None] * ea[pl.ds(t0, cs)]
        for j in range(1, CONV_A):
            xc = xc + caw_ref[j:j + 1, :][None] * ea[pl.ds(t0 + j, cs)]
        xc = (xc + cab_ref[...][None]).reshape(rows, w)
        xcb = xc.astype(BF16)
        r = jax.nn.sigmoid(_dot(xcb, wa_ref[...]) + ba_ref[...])
        gi = jax.nn.sigmoid(_dot(xcb, wx_ref[...]) + bx_ref[...])
        log_a = coef * r
        a = jnp.exp(log_a)
        a3 = a.reshape(cs, nb, w)
        b3 = (jnp.sqrt(jnp.tanh(-log_a) * (a * a + 1.0)) * (gi * xc)).reshape(cs, nb, w)
        h = h_sc[...]
        hs = []
        for t in range(cs):
            h = a3[t] * h + b3[t]
            hs.append(h[None])
        h_sc[...] = h
        hseq = jnp.concatenate(hs, axis=0).reshape(rows, w)
        y_ref[pl.ds(r0, rows), 0:w] = (hseq * jax.nn.gelu(ga_ref[pl.ds(r0, rows), :])).astype(y_ref.dtype)

        u = gc_ref[pl.ds(r0, rows), :] * xb_ref[pl.ds(r0, rows), :]
        eb[pl.ds(t0 + CONV_B - 1, cs)] = u.reshape(cs, nb, w)
        cb = cbw_ref[0:1, :][None] * eb[pl.ds(t0, cs)]
        for j in range(1, CONV_B):
            cb = cb + cbw_ref[j:j + 1, :][None] * eb[pl.ds(t0 + j, cs)]
        y_ref[pl.ds(r0, rows), w:2 * w] = (gb_ref[pl.ds(r0, rows), :] * cb.reshape(rows, w)).astype(y_ref.dtype)

        e = ep[pl.ds(t0, cs + POOL_BUF)]
        s2 = e[1:] + e[:-1]
        s4 = s2[2:] + s2[:-2]
        s8 = s4[4:] + s4[:-4]
        s16 = s8[8:] + s8[:-8]
        win = jnp.where(group == 0, s2[14:], jnp.where(group == 1, s4[12:], jnp.where(group == 2, s8[8:], s16)))
        d = (win / cnt_ref[pl.ds(t0, cs)] - e[POOL_BUF:]).reshape(rows, w)
        y_ref[pl.ds(r0, rows), 2 * w:3 * w] = (_dot(d.astype(BF16), wp_ref[...]) * ps_ref[...]).astype(y_ref.dtype)
        return carry

    lax.fori_loop(0, tt // cs, chunk, 0)

    ho_ref[...] = h_sc[...]
    ao_ref[...] = ea[tt:tt + CONV_A - 1]
    bo_ref[...] = eb[tt:tt + CONV_B - 1]
    po_ref[...] = ep[tt:tt + POOL_BUF]


def _seqmix_call(mix, cnt, states, state_layer, wts, layer, nb, tt, cs):
    n = mix.shape[1]
    t_total = n // nb
    w = W_GROUP
    assert t_total % tt == 0 and tt % cs == 0 and (tt >= POOL_BUF or tt == t_total)
    plane = lambda c: pl.BlockSpec((None, tt * nb, w), lambda i, c=c: (c, i, 0))
    entry = lambda a, j: pl.BlockSpec((None,) + a.shape[1:], lambda i: (j,) + (0,) * (a.ndim - 1))
    whole = lambda a: pl.BlockSpec(a.shape, lambda i: (0,) * len(a.shape))
    state_shapes = [jax.ShapeDtypeStruct(a.shape[1:], F32) for a in states]
    return pl.pallas_call(
        functools.partial(_seqmix_kernel, tt=tt, nb=nb, cs=cs), grid=(t_total // tt,),
        in_specs=[plane(c) for c in range(mix.shape[0])] + [pl.BlockSpec((tt, 1, w), lambda i: (i, 0, 0))]
        + [entry(a, state_layer) for a in states] + [entry(a, layer) for a in wts],
        out_specs=[pl.BlockSpec((tt * nb, 3 * w), lambda i: (i, 0))] + [whole(a) for a in state_shapes],
        out_shape=[jax.ShapeDtypeStruct((n, 3 * w), BF16)] + state_shapes,
        scratch_shapes=[pltpu.VMEM((tt + CONV_A - 1, nb, w), F32), pltpu.VMEM((tt + CONV_B - 1, nb, w), F32),
                        pltpu.VMEM((tt + POOL_BUF, nb, w), F32), pltpu.VMEM((nb, w), F32)],
        compiler_params=_cparams("arbitrary"), name="seq_mixers",
    )(*([mix] * mix.shape[0]), cnt, *states, *wts)


def _attn_kernel(bias_ref, pt_ref, q_ref, k_ref, v_ref, tri_ref, qkvs_ref, trip_ref, ckt_ref, cvt_ref, _after_in,
                 o_ref, os_ref, _after_out, kb, vb, acc_sc, c_sc, kpage, vpage, kt_sc, vt_sc, sem,
                 *, bq, layer, nact):
    i = pl.program_id(1)
    hd = HEAD_DIM
    t = k_ref.shape[0]
    heads = range(N_HEADS)
    pad = kb.shape[2] - hd - 2

    step = pl.program_id(0) * pl.num_programs(1) + i
    nsamp, tq, _ = qkvs_ref.shape
    npg = kpage.shape[1] // nsamp

    def page_copies(st, slot):
        copies = []
        for k in range(nsamp):
            for j in range(npg):
                page = pt_ref[st * nsamp + k, j]
                copies.append(pltpu.make_async_copy(ckt_ref.at[layer, page], kpage.at[slot, k * npg + j], sem.at[slot, 0]))
                copies.append(pltpu.make_async_copy(cvt_ref.at[layer, page], vpage.at[slot, k * npg + j], sem.at[slot, 1]))
        return copies

    @pl.when(step == 0)
    def _():
        for cp in page_copies(0, 0):
            cp.start()

    @pl.when(step + 1 < nact)
    def _():
        for cp in page_copies(step + 1, (step + 1) % 2):
            cp.start()

    @pl.when(step < nact)
    def _():
        slot = step % 2
        for cp in page_copies(step, slot):
            cp.wait()
        for k in range(nsamp):
            pages = [(kpage.at[slot, k * npg + j], vpage.at[slot, k * npg + j]) for j in range(npg)]
            os_ref[k] = _paged_sample(bias_ref, qkvs_ref[k], pages, trip_ref[...], kt_sc, vt_sc)

    @pl.when(i == 0)
    def _():
        for h in heads:
            b = jnp.full((t, 1), bias_ref[h], F32)
            b_hi = b.astype(BF16)
            b_lo = (b - b_hi.astype(F32)).astype(BF16)
            kb[h] = jnp.concatenate(
                [k_ref[:, h * hd:(h + 1) * hd].astype(BF16), b_hi, b_lo, jnp.zeros((t, pad), BF16)], axis=1)
            vb[h] = v_ref[:, h * hd:(h + 1) * hd].astype(BF16)

    tri = tri_ref[...]
    q_tail = jnp.concatenate([jnp.ones((bq, 2), BF16), jnp.zeros((bq, pad), BF16)], axis=1)
    qs = [jnp.concatenate([(q_ref[:, h * hd:(h + 1) * hd] * SB_SCALE).astype(BF16), q_tail], axis=1) for h in heads]
    causal = lax.broadcasted_iota(jnp.int32, (bq, bq), 1) < lax.broadcasted_iota(jnp.int32, (bq, bq), 0)

    def sweep(blocks, first):
        rows = [pl.ds(pl.multiple_of(j * bq, bq), bq) for j, _ in blocks]
        units = [(n, h) for n in range(len(blocks)) for h in heads]
        carry = [None if first else c_sc[h] for h in heads]
        pvs = [None] * N_HEADS
        z, ls, col0, later = {}, {}, {}, {}
        for k in range(len(units) + 2):
            if k < len(units):
                n, h = units[k]
                z[k] = _dot_nt(qs[h], kb[h, rows[n], :])
            if 1 <= k <= len(units):
                mask = blocks[units[k - 1][0]][1]
                zz = z.pop(k - 1)
                sp = jnp.maximum(zz, 0.0) + jnp.log(1.0 + jnp.exp2(jnp.abs(zz) * -LOG2E))
                ls[k - 1] = zz - sp
                if mask is not None:
                    sp = jnp.where(mask, sp, 0.0)
                col0[k - 1] = sp[:, 0:1]
                later[k - 1] = _dot(sp.astype(BF16), tri)
            if 2 <= k:
                n, h = units[k - 2]
                mask = blocks[n][1]
                lt = later.pop(k - 2)
                e = ls.pop(k - 2) - lt
                if carry[h] is not None:
                    e = e - carry[h]
                p = jnp.exp(e)
                if mask is not None:
                    p = jnp.where(mask, p, 0.0)
                total = lt[:, 0:1] + col0.pop(k - 2)
                carry[h] = total if carry[h] is None else carry[h] + total
                pv = _dot(p.astype(BF16), vb[h, rows[n], :])
                pvs[h] = pv if pvs[h] is None else pvs[h] + pv
        for h in heads:
            c_sc[h] = carry[h]
            acc_sc[h] = pvs[h] if first else acc_sc[h] + pvs[h]

    @pl.when(i % 2 == 0)
    def _():
        sweep([(i, causal)], True)

    @pl.when(i % 2 == 1)
    def _():
        sweep([(i, causal), (i - 1, None)], True)

    top = i - 1 - i % 2

    def body(pair, carry):
        j = top - 2 * pair
        sweep([(j, None), (j - 1, None)], False)
        return carry

    lax.fori_loop(0, i // 2, body, 0)
    o_ref[...] = jnp.concatenate([acc_sc[h] for h in heads], axis=-1)


def _attn_call(qkv, qkv_s, cache_kt, cache_vt, page_table, sb_bias, tri, tri_page, layer, bq, after):
    nb, t, _ = qkv.shape
    ns, tq, _ = qkv_s.shape
    npg = page_table.shape[1]
    w = W_GROUP
    nsteps = nb * (t // bq)
    nsamp = -(-ns // nsteps)
    nact = ns // nsamp
    assert ns % nsamp == 0 and (N_HEADS * tq) % 8 == 0
    sample_block = lambda b, i, s, p: (jnp.minimum(b * (t // bq) + i, nact - 1), 0, 0)
    page_ring = pltpu.VMEM((2, nsamp * npg, N_HEADS, HEAD_DIM, PAGE_SIZE), F32)
    grid_spec = pltpu.PrefetchScalarGridSpec(
        num_scalar_prefetch=2, grid=(nb, t // bq),
        in_specs=[pl.BlockSpec((None, bq, w), lambda b, i, s, p: (b, i, 0)),
                  pl.BlockSpec((None, t, w), lambda b, i, s, p: (b, 0, 1)),
                  pl.BlockSpec((None, t, w), lambda b, i, s, p: (b, 0, 2)),
                  pl.BlockSpec((bq, bq), lambda b, i, s, p: (0, 0)),
                  pl.BlockSpec((nsamp, tq, 3 * w), sample_block),
                  pl.BlockSpec((PAGE_SIZE, PAGE_SIZE), lambda b, i, s, p: (0, 0)),
                  pl.BlockSpec(memory_space=pl.ANY), pl.BlockSpec(memory_space=pl.ANY),
                  pl.BlockSpec(memory_space=pl.ANY)],
        out_specs=[pl.BlockSpec((None, bq, w), lambda b, i, s, p: (b, i, 0)),
                   pl.BlockSpec((nsamp, tq, w), sample_block), pl.BlockSpec(memory_space=pl.ANY)],
        scratch_shapes=[pltpu.VMEM((N_HEADS, t, LANES), BF16), pltpu.VMEM((N_HEADS, t, HEAD_DIM), BF16),
                        pltpu.VMEM((N_HEADS, bq, HEAD_DIM), F32), pltpu.VMEM((N_HEADS, bq, 1), F32),
                        page_ring, page_ring,
                        pltpu.VMEM((w, npg * PAGE_SIZE), BF16), pltpu.VMEM((w, npg * PAGE_SIZE), BF16),
                        pltpu.SemaphoreType.DMA((2, 2))])
    return pl.pallas_call(
        functools.partial(_attn_kernel, bq=bq, layer=layer, nact=nact), grid_spec=grid_spec,
        out_shape=[jax.ShapeDtypeStruct((nb, t, w), F32), jax.ShapeDtypeStruct((ns, tq, w), F32),
                   jax.ShapeDtypeStruct(after.shape, after.dtype)],
        input_output_aliases={10: 2},
        compiler_params=_cparams("arbitrary", "arbitrary"), name="sb_attention",
    )(sb_bias, page_table, qkv, qkv, qkv, tri, qkv_s, tri_page, cache_kt, cache_vt, after)


def _paged_sample(bias_ref, qkv, pages, tri, kt_sc, vt_sc):
    npg = len(pages)
    tq = qkv.shape[0]
    hd, w, ps = HEAD_DIM, W_GROUP, PAGE_SIZE
    nr = N_HEADS * tq
    for j, (k_ref, v_ref) in enumerate(pages):
        kt_sc[:, j * ps:(j + 1) * ps] = k_ref[...].reshape(w, ps).astype(BF16)
        vt_sc[:, j * ps:(j + 1) * ps] = v_ref[...].reshape(w, ps).astype(BF16)

    q, kn, vn = qkv[:, 0:w], qkv[:, w:2 * w], qkv[:, 2 * w:3 * w]
    lane_head = lax.broadcasted_iota(jnp.int32, (1, w), 1) // hd
    row = lax.broadcasted_iota(jnp.int32, (nr, 1), 0)
    row_head, row_t = row // tq, row % tq
    qbd = jnp.concatenate([jnp.where(lane_head == h, q, 0.0) for h in range(N_HEADS)], axis=0)
    bias = jnp.zeros((nr, 1), F32)
    for h in range(N_HEADS):
        bias = jnp.where(row_head == h, bias_ref[h], bias)

    run = jnp.zeros((nr, 1), F32)
    acc = jnp.zeros((nr, w), F32)
    for j in range(tq - 1, -1, -1):
        z = jnp.sum(qbd * kn[j:j + 1, :], axis=-1, keepdims=True) * SB_SCALE + bias
        seen = j < row_t
        run = run + jnp.where(seen, _softplus(z), 0.0)
        acc = acc + jnp.where(seen, jnp.exp(z - run), 0.0) * vn[j:j + 1, :]

    z_all = _dot(qbd.astype(BF16), kt_sc[...]) * SB_SCALE + bias
    z2 = jnp.concatenate([z_all[:, j * ps:(j + 1) * ps] for j in range(npg)], axis=0)
    in_page = _suffix_sum(_softplus(z2), tri)
    later = [None] * npg
    for j in range(npg - 1, -1, -1):
        later[j] = run
        run = run + in_page[j * nr:(j + 1) * nr, 0:1]
    p2 = jnp.exp(z2 - (in_page + jnp.concatenate(later, axis=0)))
    p_all = jnp.concatenate([p2[j * nr:(j + 1) * nr] for j in range(npg)], axis=1).astype(BF16)
    acc = acc + _dot_nt(p_all, vt_sc[...])
    out = jnp.where(lane_head == 0, acc[0:tq], 0.0)
    for h in range(1, N_HEADS):
        out = jnp.where(lane_head == h, acc[h * tq:(h + 1) * tq], out)
    return out


def _block_diag(wt):
    depth, nh, d, _ = wt.shape
    return jnp.einsum("lhij,hg->lhigj", wt, jnp.eye(nh, dtype=wt.dtype)).reshape(depth, nh * d, nh * d)


def _window_counts(p0, t):
    win = jnp.repeat(jnp.asarray(POOL_WINDOWS, jnp.int32), POOL_GROUP)
    pos = p0 + jnp.arange(t, dtype=jnp.int32)
    return jnp.minimum(win[None, :], pos[:, None] + 1).astype(F32).reshape(t, 1, W_GROUP)


def _to_time_major(a):
    return jnp.swapaxes(a, 0, 1)


def kernel(x_prompt, x_sample, cache_k, cache_v, state_rglru_h, state_conv_a, state_conv_b, state_pool, page_table,
           ln_in_g, ln_in_b, w_in, conv_a_w, conv_a_b, lru_wa, lru_ba, lru_wx, lru_bx, lru_lambda, conv_b_w,
           sb_bias, pool_w, pool_scale, w_out, ln1_g, ln1_b, w_ff1, w_ff2, ln2_g, ln2_b):
    depth = w_in.shape[0]
    alpha = (2.0 * depth) ** 0.25
    bp, tp, d = x_prompt.shape
    bs, ts, _ = x_sample.shape
    past_len = page_table.shape[1] * PAGE_SIZE
    w = W_GROUP

    tm_p = min(512, bp * tp)
    tm_s = min(512, bs * ts)
    bq = min(256, tp)
    tt_p = min(256, tp)
    cs_p = min(16, tt_p)

    cnt_p = _window_counts(0, tp)
    cnt_s = _window_counts(past_len, ts)
    tri_q = jnp.tril(jnp.ones((bq, bq), BF16), k=-1)
    tri_page = jnp.tril(jnp.ones((PAGE_SIZE, PAGE_SIZE), BF16))
    cache_kt = jnp.transpose(cache_k, (0, 1, 3, 4, 2))
    cache_vt = jnp.transpose(cache_v, (0, 1, 3, 4, 2))
    zeros = lambda *shape: jnp.zeros(shape, F32)

    nmix = (N_PROJ_GROUPS - 3) * w
    w_in_b = jnp.concatenate([w_in[..., :5 * w], w_in[..., 8 * w:], w_in[..., 5 * w:8 * w]], axis=-1).astype(BF16)
    stack_row = lambda a: a.reshape(depth, 1, -1)
    dense_w = (jnp.concatenate([w_out[:, :2 * w], w_out[:, 3 * w:]], axis=1).astype(BF16),
               w_out[:, 2 * w:3 * w].astype(BF16), stack_row(ln1_g), stack_row(ln1_b),
               w_ff1.astype(BF16), w_ff2.astype(BF16), stack_row(ln2_g), stack_row(ln2_b), w_in_b)

    xp, mix_p, qkv_p, k_stack, v_stack = _ln_inproj_call(x_prompt, ln_in_g, ln_in_b, w_in_b, nmix, bp, tp, tm_p, True)
    xs, mix_s, qkv_s = _ln_inproj_call(
        _to_time_major(x_sample).reshape(ts * bs, d), ln_in_g, ln_in_b, w_in_b, nmix, bs, ts, tm_s, False)

    mix_w = (conv_a_w, stack_row(conv_a_b), _block_diag(lru_wa).astype(BF16), stack_row(lru_ba),
             _block_diag(lru_wx).astype(BF16), stack_row(lru_bx), stack_row(lru_lambda), conv_b_w,
             _block_diag(pool_w).astype(BF16), stack_row(pool_scale))
    states_p = (zeros(1, bp, w), zeros(1, CONV_A - 1, bp, w), zeros(1, CONV_B - 1, bp, w), zeros(1, POOL_BUF, bp, w))
    states_s = (state_rglru_h, jnp.swapaxes(state_conv_a, 1, 2), jnp.swapaxes(state_conv_b, 1, 2),
                jnp.swapaxes(state_pool, 1, 2))

    outs = {name: [] for name in ("ks", "vs", "hp", "hs", "ap", "as", "bp", "bs", "pp", "ps")}
    for l in range(depth):
        qkv = _to_time_major(qkv_s.reshape(ts, bs, 3 * w))
        yc_p, yc, mix_p = _attn_call(
            qkv_p, qkv, cache_kt, cache_vt, page_table, sb_bias[l], tri_q, tri_page, l, bq, mix_p)
        yc = _to_time_major(yc).reshape(ts * bs, w)

        yabd_p, h_n, a_n, b_n, p_n = _seqmix_call(mix_p, cnt_p, states_p, 0, mix_w, l, bp, tt_p, cs_p)
        outs["hp"].append(h_n)
        outs["ap"].append(a_n)
        outs["bp"].append(b_n)
        outs["pp"].append(p_n)
        yabd, h_n, a_n, b_n, p_n = _seqmix_call(mix_s, cnt_s, states_s, l, mix_w, l, bs, ts, 1)

        res = _dense_call(yabd_p, yc_p, xp, dense_w, (k_stack, v_stack), l, nmix, bp, tp, tm_p, alpha, True)
        if l == depth - 1:
            y_prompt = res
        else:
            xp, mix_p, qkv_p, k_stack, v_stack = res
        res = _dense_call(yabd, yc, xs, dense_w, None, l, nmix, bs, ts, tm_s, alpha, False)
        if l == depth - 1:
            y_sample = _to_time_major(res.reshape(ts, bs, d))
        else:
            xs, mix_s, qkv_s = res
        outs["ks"].append(qkv[..., w:2 * w].reshape(bs, ts, N_HEADS, HEAD_DIM))
        outs["vs"].append(qkv[..., 2 * w:].reshape(bs, ts, N_HEADS, HEAD_DIM))
        outs["hs"].append(h_n)
        outs["as"].append(a_n)
        outs["bs"].append(b_n)
        outs["ps"].append(p_n)

    st = {name: jnp.stack(v) for name, v in outs.items()}
    heads = lambda a: a.reshape(depth, bp, tp, N_HEADS, HEAD_DIM)
    seq_first = lambda a: jnp.swapaxes(a, 1, 2)
    return (y_prompt, y_sample, heads(k_stack), heads(v_stack), st["ks"], st["vs"], st["hp"], st["hs"],
            seq_first(st["ap"]), seq_first(st["as"]), seq_first(st["bp"]), seq_first(st["bs"]),
            seq_first(st["pp"]), seq_first(st["ps"]))
```

```python
import functools

import jax
import jax.numpy as jnp
from jax import lax
from jax.experimental import pallas as pl
from jax.experimental.pallas import tpu as pltpu

F32 = jnp.float32
BF16 = jnp.bfloat16

N_HEADS = 4
W_GROUP = 256
HEAD_DIM = W_GROUP // N_HEADS
N_PROJ_GROUPS = 9
CONV_A = 4
CONV_B = 3
RGLRU_C = 8.0
POOL_WINDOWS = (2, 4, 8, 16)
POOL_BUF = max(POOL_WINDOWS) - 1
POOL_GROUP = W_GROUP // len(POOL_WINDOWS)
PAGE_SIZE = 128
LN_EPS = 1e-5
SB_SCALE = HEAD_DIM ** -0.5
LANES = 128
LOG2E = 1.4426950408889634
PAGE_RING_SLOTS = 3

VMEM_LIMIT_BYTES = 56 * 1024 * 1024


def _cparams(*sem):
    return pltpu.CompilerParams(dimension_semantics=sem, vmem_limit_bytes=VMEM_LIMIT_BYTES)


def _layer_norm(x, g, b):
    mu = jnp.mean(x, axis=-1, keepdims=True)
    xc = x - mu
    var = jnp.mean(xc * xc, axis=-1, keepdims=True)
    return xc * lax.rsqrt(var + LN_EPS) * g + b


def _softplus(x):
    return jnp.maximum(x, 0.0) + jnp.log1p(jnp.exp(-jnp.abs(x)))


def _dot(a, b):
    return jnp.dot(a, b, preferred_element_type=F32)


def _dot_nt(a, b):
    return lax.dot_general(a, b, (((1,), (1,)), ((), ())), preferred_element_type=F32)


def _suffix_sum(x, tri):
    hi = x.astype(BF16)
    lo = (x - hi.astype(F32)).astype(BF16)
    return _dot(hi, tri) + _dot(lo, tri)


def _rows_from_blocks(src_ref, slab_ref, fn=lambda v: v):
    nb, tt, c = src_ref.shape
    for s in range(nb):
        v = fn(src_ref[s])
        for j in range(c // LANES):
            slab_ref[j, pl.ds(s, tt, stride=nb), :] = v[:, j * LANES:(j + 1) * LANES]
    return jnp.concatenate([slab_ref[j] for j in range(c // LANES)], axis=1)


def _blocks_from_rows(val, slab_ref, dst_ref, copies=()):
    nb, tt, c = dst_ref.shape
    for j in range(c // LANES):
        slab_ref[j] = val[:, j * LANES:(j + 1) * LANES]
    for s in range(nb):
        for j in range(c // LANES):
            piece = slab_ref[j, pl.ds(s, tt, stride=nb), :]
            dst_ref[s, :, j * LANES:(j + 1) * LANES] = piece
            for ref, c0 in copies:
                if c0 <= j * LANES < c0 + ref.shape[-1]:
                    ref[s, :, j * LANES - c0:(j + 1) * LANES - c0] = piece


def _emit_inproj(xb, w_ref, mix_ref, qkv_ref, kv_refs, slab_ref):
    ngroups, _, w = mix_ref.shape
    nmix = ngroups * w
    mix = _dot(xb, w_ref[:, :nmix])
    for grp in range(ngroups):
        mix_ref[grp] = mix[:, grp * w:(grp + 1) * w]
    if slab_ref is None:
        qkv_ref[...] = _dot(xb, w_ref[:, nmix:])
    else:
        k_ref, v_ref = kv_refs
        _blocks_from_rows(_dot(xb, w_ref[:, nmix:]), slab_ref, qkv_ref, ((k_ref, W_GROUP), (v_ref, 2 * W_GROUP)))


def _ln_inproj_kernel(x_ref, g_ref, b_ref, w_ref, *rest, batch_major):
    norm = lambda v: _layer_norm(v, g_ref[...], b_ref[...])
    if batch_major:
        xo_ref, mix_ref, qkv_ref, k_ref, v_ref, slab_x, slab_qkv = rest[2:]
        x0 = _rows_from_blocks(x_ref, slab_x, norm)
        xo_ref[...] = x0
        _emit_inproj(x0.astype(BF16), w_ref, mix_ref, qkv_ref, (k_ref, v_ref), slab_qkv)
    else:
        xo_ref, mix_ref, qkv_ref = rest
        x0 = norm(x_ref[...])
        xo_ref[...] = x0
        _emit_inproj(x0.astype(BF16), w_ref, mix_ref, qkv_ref, None, None)


def _dense_kernel(yabd_ref, yc_ref, x_ref, wabd_ref, wc_ref, g1_ref, b1_ref, w1_ref, w2_ref, g2_ref, b2_ref,
                  *rest, alpha, fc, batch_major, last):
    kv_refs = None
    if last:
        (y_ref, acc_ref), scratch = rest[:2], rest[2:]
    elif batch_major:
        win_ref, xo_ref, mix_ref, qkv_ref, k_ref, v_ref, acc_ref = rest[:1] + rest[3:9]
        kv_refs, scratch = (k_ref, v_ref), rest[9:]
    else:
        (win_ref, xo_ref, mix_ref, qkv_ref, acc_ref), scratch = rest[:5], rest[5:]
    yc = _rows_from_blocks(yc_ref, scratch[0]) if batch_major else yc_ref[...]
    mix = _dot(yabd_ref[...], wabd_ref[...]) + _dot(yc.astype(BF16), wc_ref[...])
    x1 = _layer_norm(alpha * x_ref[...] + mix, g1_ref[...], b1_ref[...])
    xb = x1.astype(BF16)
    for c in range(w1_ref.shape[1] // fc):
        h = jnp.maximum(_dot(xb, w1_ref[:, c * fc:(c + 1) * fc]), 0.0)
        part = _dot((h * h).astype(BF16), w2_ref[c * fc:(c + 1) * fc, :])
        if c == 0:
            acc_ref[...] = part
        else:
            acc_ref[...] += part
    x2 = _layer_norm(alpha * x1 + acc_ref[...], g2_ref[...], b2_ref[...])
    if not last:
        xo_ref[...] = x2
        _emit_inproj(x2.astype(BF16), win_ref, mix_ref, qkv_ref, kv_refs, scratch[1] if batch_major else None)
    elif batch_major:
        _blocks_from_rows(x2, scratch[1], y_ref)
    else:
        y_ref[...] = x2


def _resident(shape, index_map):
    return pl.BlockSpec(shape, index_map, pipeline_mode=pl.Buffered(1))


def _layer_block(a, layer):
    return _resident((None,) + a.shape[1:], lambda i: (layer, 0, 0))


def _slab(rows, c):
    return pltpu.VMEM((c // LANES, rows, LANES), F32)


def _row_specs(nb, t, tm, batch_major):
    rows = lambda c: (pl.BlockSpec((tm, c), lambda i: (i, 0)), (nb * t, c))
    if batch_major:
        blocks = lambda c: (pl.BlockSpec((nb, tm // nb, c), lambda i: (0, i, 0)), (nb, t, c))
    else:
        blocks = rows
    return rows, blocks


def _planes_spec(n, tm, c):
    return (pl.BlockSpec((c // W_GROUP, tm, W_GROUP), lambda i: (0, i, 0)), (c // W_GROUP, n, W_GROUP))


def _kv_stack_spec(depth, nb, t, tm, layer):
    return (pl.BlockSpec((None, nb, tm // nb, W_GROUP), lambda i: (layer, 0, i, 0)), (depth, nb, t, W_GROUP))


def _ln_inproj_call(x, g, b, w_in, nmix, nb, t, tm, batch_major):
    d = x.shape[-1]
    depth = w_in.shape[0]
    nqkv = w_in.shape[2] - nmix
    rows, blocks = _row_specs(nb, t, tm, batch_major)
    vec = _resident((1, d), lambda i: (0, 0))
    in_specs = [blocks(d)[0], vec, vec, _layer_block(w_in, 0)]
    operands = (x, g.reshape(1, d), b.reshape(1, d), w_in)
    out = [rows(d), _planes_spec(nb * t, tm, nmix), blocks(nqkv)]
    aliases = {}
    if batch_major:
        aliases = {len(operands): 3, len(operands) + 1: 4}
        in_specs += [pl.BlockSpec(memory_space=pl.ANY)] * 2
        operands += (jnp.zeros((depth, nb, t, W_GROUP), F32), jnp.zeros((depth, nb, t, W_GROUP), F32))
        out += [_kv_stack_spec(depth, nb, t, tm, 0)] * 2
    return pl.pallas_call(
        functools.partial(_ln_inproj_kernel, batch_major=batch_major), grid=(nb * t // tm,),
        in_specs=in_specs, out_specs=[s for s, _ in out],
        out_shape=[jax.ShapeDtypeStruct(shp, F32) for _, shp in out],
        scratch_shapes=[_slab(tm, d), _slab(tm, nqkv)] if batch_major else [],
        input_output_aliases=aliases, compiler_params=_cparams("arbitrary"), name="ln_in_proj")(*operands)


def _dense_call(yabd, yc, x, wts, kv_stacks, layer, nmix, nb, t, tm, alpha, batch_major, fc=512):
    d = x.shape[-1]
    w_in = wts[-1]
    depth = w_in.shape[0]
    last = layer == depth - 1
    nqkv = w_in.shape[2] - nmix
    rows, blocks = _row_specs(nb, t, tm, batch_major)
    in_specs = [rows(yabd.shape[-1])[0], blocks(yc.shape[-1])[0], rows(d)[0]] + [_layer_block(a, layer) for a in wts[:-1]]
    scratch = [pltpu.VMEM((tm, d), F32)]
    aliases = {}
    if batch_major:
        scratch += [_slab(tm, yc.shape[-1]), _slab(tm, d if last else nqkv)]
    if last:
        out = [blocks(d)]
        operands = (yabd, yc, x) + tuple(wts[:-1])
    else:
        in_specs.append(_layer_block(w_in, layer + 1))
        out = [rows(d), _planes_spec(nb * t, tm, nmix), blocks(nqkv)]
        operands = (yabd, yc, x) + tuple(wts)
        if batch_major:
            aliases = {len(operands): 3, len(operands) + 1: 4}
            in_specs += [pl.BlockSpec(memory_space=pl.ANY)] * 2
            out += [_kv_stack_spec(depth, nb, t, tm, layer + 1)] * 2
            operands += tuple(kv_stacks)
    res = pl.pallas_call(
        functools.partial(_dense_kernel, alpha=alpha, fc=fc, batch_major=batch_major, last=last),
        grid=(nb * t // tm,), in_specs=in_specs,
        out_specs=[s for s, _ in out], out_shape=[jax.ShapeDtypeStruct(shp, F32) for _, shp in out],
        scratch_shapes=scratch, input_output_aliases=aliases,
        compiler_params=_cparams("arbitrary"), name="dense_layer")(*operands)
    return res[0] if last else res


def _seqmix_kernel(xa_ref, ga_ref, xb_ref, gb_ref, gc_ref, xp_ref, cnt_ref,
                   h0_ref, bufa_ref, bufb_ref, bufp_ref,
                   caw_ref, cab_ref, wa_ref, ba_ref, wx_ref, bx_ref, lam_ref, cbw_ref, wp_ref, ps_ref,
                   y_ref, ho_ref, ao_ref, bo_ref, po_ref,
                   ea, eb, ep, h_sc, *, tt, nb, cs):
    w = W_GROUP
    i = pl.program_id(0)

    @pl.when(i == 0)
    def _():
        ea[0:CONV_A - 1] = bufa_ref[...]
        eb[0:CONV_B - 1] = bufb_ref[...]
        ep[0:POOL_BUF] = bufp_ref[...]
        h_sc[...] = h0_ref[...]

    @pl.when(i > 0)
    def _():
        ea[0:CONV_A - 1] = ea[tt:tt + CONV_A - 1]
        eb[0:CONV_B - 1] = eb[tt:tt + CONV_B - 1]
        ep[0:POOL_BUF] = ep[tt:tt + POOL_BUF]

    ea[CONV_A - 1:CONV_A - 1 + tt] = xa_ref[...].reshape(tt, nb, w)
    ep[POOL_BUF:POOL_BUF + tt] = xp_ref[...].reshape(tt, nb, w)

    coef = -RGLRU_C * _softplus(-lam_ref[...])
    group = lax.broadcasted_iota(jnp.int32, (1, 1, w), 2) // POOL_GROUP
    rows = cs * nb

    def chunk(k, carry):
        t0 = pl.multiple_of(k * cs, cs)
        r0 = pl.multiple_of(k * rows, rows)

        xc = caw_ref[0:1, :][None] * ea[pl.ds(t0, cs)]
        for j in range(1, CONV_A):
            xc = xc + caw_ref[j:j + 1, :][None] * ea[pl.ds(t0 + j, cs)]
        xc = (xc + cab_ref[...][None]).reshape(rows, w)
        xcb = xc.astype(BF16)
        r = jax.nn.sigmoid(_dot(xcb, wa_ref[...]) + ba_ref[...])
        gi = jax.nn.sigmoid(_dot(xcb, wx_ref[...]) + bx_ref[...])
        log_a = coef * r
        a = jnp.exp(log_a)
        a3 = a.reshape(cs, nb, w)
        b3 = (jnp.sqrt(jnp.tanh(-log_a) * (a * a + 1.0)) * (gi * xc)).reshape(cs, nb, w)
        h = h_sc[...]
        hs = []
        for t in range(cs):
            h = a3[t] * h + b3[t]
            hs.append(h[None])
        h_sc[...] = h
        hseq = jnp.concatenate(hs, axis=0).reshape(rows, w)
        y_ref[pl.ds(r0, rows), 0:w] = (hseq * jax.nn.gelu(ga_ref[pl.ds(r0, rows), :])).astype(y_ref.dtype)

        u = gc_ref[pl.ds(r0, rows), :] * xb_ref[pl.ds(r0, rows), :]
        eb[pl.ds(t0 + CONV_B - 1, cs)] = u.reshape(cs, nb, w)
        cb = cbw_ref[0:1, :][None] * eb[pl.ds(t0, cs)]
        for j in range(1, CONV_B):
            cb = cb + cbw_ref[j:j + 1, :][None] * eb[pl.ds(t0 + j, cs)]
        y_ref[pl.ds(r0, rows), w:2 * w] = (gb_ref[pl.ds(r0, rows), :] * cb.reshape(rows, w)).astype(y_ref.dtype)

        e = ep[pl.ds(t0, cs + POOL_BUF)]
        s2 = e[1:] + e[:-1]
        s4 = s2[2:] + s2[:-2]
        s8 = s4[4:] + s4[:-4]
        s16 = s8[8:] + s8[:-8]
        win = jnp.where(group == 0, s2[14:], jnp.where(group == 1, s4[12:], jnp.where(group == 2, s8[8:], s16)))
        d = (win / cnt_ref[pl.ds(t0, cs)] - e[POOL_BUF:]).reshape(rows, w)
        y_ref[pl.ds(r0, rows), 2 * w:3 * w] = (_dot(d.astype(BF16), wp_ref[...]) * ps_ref[...]).astype(y_ref.dtype)
        return carry

    lax.fori_loop(0, tt // cs, chunk, 0)

    ho_ref[...] = h_sc[...]
    ao_ref[...] = ea[tt:tt + CONV_A - 1]
    bo_ref[...] = eb[tt:tt + CONV_B - 1]
    po_ref[...] = ep[tt:tt + POOL_BUF]


def _seqmix_call(mix, cnt, states, state_layer, wts, layer, nb, tt, cs):
    n = mix.shape[1]
    t_total = n // nb
    w = W_GROUP
    assert t_total % tt == 0 and tt % cs == 0 and (tt >= POOL_BUF or tt == t_total)
    plane = lambda c: pl.BlockSpec((None, tt * nb, w), lambda i, c=c: (c, i, 0))
    entry = lambda a, j: pl.BlockSpec((None,) + a.shape[1:], lambda i: (j,) + (0,) * (a.ndim - 1))
    whole = lambda a: pl.BlockSpec(a.shape, lambda i: (0,) * len(a.shape))
    state_shapes = [jax.ShapeDtypeStruct(a.shape[1:], F32) for a in states]
    return pl.pallas_call(
        functools.partial(_seqmix_kernel, tt=tt, nb=nb, cs=cs), grid=(t_total // tt,),
        in_specs=[plane(c) for c in range(mix.shape[0])] + [pl.BlockSpec((tt, 1, w), lambda i: (i, 0, 0))]
        + [entry(a, state_layer) for a in states] + [entry(a, layer) for a in wts],
        out_specs=[pl.BlockSpec((tt * nb, 3 * w), lambda i: (i, 0))] + [whole(a) for a in state_shapes],
        out_shape=[jax.ShapeDtypeStruct((n, 3 * w), BF16)] + state_shapes,
        scratch_shapes=[pltpu.VMEM((tt + CONV_A - 1, nb, w), F32), pltpu.VMEM((tt + CONV_B - 1, nb, w), F32),
                        pltpu.VMEM((tt + POOL_BUF, nb, w), F32), pltpu.VMEM((nb, w), F32)],
        compiler_params=_cparams("arbitrary"), name="seq_mixers",
    )(*([mix] * mix.shape[0]), cnt, *states, *wts)


def _attn_kernel(bias_ref, pt_ref, q_ref, k_ref, v_ref, tri_ref, qkvs_ref, trip_ref, ckt_ref, cvt_ref,
                 o_ref, os_ref, kb, vb, acc_sc, c_sc, kpage, vpage, kt_sc, vt_sc, sem, *, bq, layer, nact):
    i = pl.program_id(1)
    hd = HEAD_DIM
    t = k_ref.shape[0]
    heads = range(N_HEADS)
    pad = kb.shape[2] - hd - 2

    step = pl.program_id(0) * pl.num_programs(1) + i
    nsamp, tq, _ = qkvs_ref.shape
    nslot = kpage.shape[0]
    npg = kpage.shape[1] // nsamp

    def page_copies(st, slot):
        copies = []
        for k in range(nsamp):
            for j in range(npg):
                page = pt_ref[st * nsamp + k, j]
                copies.append(pltpu.make_async_copy(ckt_ref.at[layer, page], kpage.at[slot, k * npg + j], sem.at[slot, 0]))
                copies.append(pltpu.make_async_copy(cvt_ref.at[layer, page], vpage.at[slot, k * npg + j], sem.at[slot, 1]))
        return copies

    @pl.when(step == 0)
    def _():
        for ahead in range(min(nslot - 1, nact)):
            for cp in page_copies(ahead, ahead):
                cp.start()

    @pl.when(step + nslot - 1 < nact)
    def _():
        for cp in page_copies(step + nslot - 1, (step + nslot - 1) % nslot):
            cp.start()

    @pl.when(step < nact)
    def _():
        slot = step % nslot
        for cp in page_copies(step, slot):
            cp.wait()
        for k in range(nsamp):
            pages = [(kpage.at[slot, k * npg + j], vpage.at[slot, k * npg + j]) for j in range(npg)]
            os_ref[k] = _paged_sample(bias_ref, qkvs_ref[k], pages, trip_ref[...], kt_sc, vt_sc)

    @pl.when(i == 0)
    def _():
        for h in heads:
            b = jnp.full((t, 1), bias_ref[h], F32)
            b_hi = b.astype(BF16)
            b_lo = (b - b_hi.astype(F32)).astype(BF16)
            kb[h] = jnp.concatenate(
                [k_ref[:, h * hd:(h + 1) * hd].astype(BF16), b_hi, b_lo, jnp.zeros((t, pad), BF16)], axis=1)
            vb[h] = v_ref[:, h * hd:(h + 1) * hd].astype(BF16)

    tri = tri_ref[...]
    q_tail = jnp.concatenate([jnp.ones((bq, 2), BF16), jnp.zeros((bq, pad), BF16)], axis=1)
    qs = [jnp.concatenate([(q_ref[:, h * hd:(h + 1) * hd] * SB_SCALE).astype(BF16), q_tail], axis=1) for h in heads]
    causal = lax.broadcasted_iota(jnp.int32, (bq, bq), 1) < lax.broadcasted_iota(jnp.int32, (bq, bq), 0)

    def sweep(blocks, first):
        rows = [pl.ds(pl.multiple_of(j * bq, bq), bq) for j, _ in blocks]
        units = [(n, h) for n in range(len(blocks)) for h in heads]
        carry = [None if first else c_sc[h] for h in heads]
        pvs = [None] * N_HEADS
        z, ls, col0, later = {}, {}, {}, {}
        for k in range(len(units) + 2):
            if k < len(units):
                n, h = units[k]
                z[k] = _dot_nt(qs[h], kb[h, rows[n], :])
            if 1 <= k <= len(units):
                mask = blocks[units[k - 1][0]][1]
                zz = z.pop(k - 1)
                sp = jnp.maximum(zz, 0.0) + jnp.log(1.0 + jnp.exp2(jnp.abs(zz) * -LOG2E))
                ls[k - 1] = zz - sp
                if mask is not None:
                    sp = jnp.where(mask, sp, 0.0)
                col0[k - 1] = sp[:, 0:1]
                later[k - 1] = _dot(sp.astype(BF16), tri)
            if 2 <= k:
                n, h = units[k - 2]
                mask = blocks[n][1]
                lt = later.pop(k - 2)
                e = ls.pop(k - 2) - lt
                if carry[h] is not None:
                    e = e - carry[h]
                p = jnp.exp(e)
                if mask is not None:
                    p = jnp.where(mask, p, 0.0)
                total = lt[:, 0:1] + col0.pop(k - 2)
                carry[h] = total if carry[h] is None else carry[h] + total
                pv = _dot(p.astype(BF16), vb[h, rows[n], :])
                pvs[h] = pv if pvs[h] is None else pvs[h] + pv
        for h in heads:
            c_sc[h] = carry[h]
            acc_sc[h] = pvs[h] if first else acc_sc[h] + pvs[h]

    @pl.when(i % 2 == 0)
    def _():
        sweep([(i, causal)], True)

    @pl.when(i % 2 == 1)
    def _():
        sweep([(i, causal), (i - 1, None)], True)

    top = i - 1 - i % 2

    def body(pair, carry):
        j = top - 2 * pair
        sweep([(j, None), (j - 1, None)], False)
        return carry

    lax.fori_loop(0, i // 2, body, 0)
    o_ref[...] = jnp.concatenate([acc_sc[h] for h in heads], axis=-1)


def _attn_call(qkv, qkv_s, cache_kt, cache_vt, page_table, sb_bias, tri, tri_page, layer, bq):
    nb, t, _ = qkv.shape
    ns, tq, _ = qkv_s.shape
    npg = page_table.shape[1]
    w = W_GROUP
    nsteps = nb * (t // bq)
    nsamp = -(-ns // nsteps)
    nact = ns // nsamp
    assert ns % nsamp == 0 and (N_HEADS * tq) % 8 == 0
    sample_block = lambda b, i, s, p: (jnp.minimum(b * (t // bq) + i, nact - 1), 0, 0)
    page_ring = pltpu.VMEM((PAGE_RING_SLOTS, nsamp * npg, N_HEADS, HEAD_DIM, PAGE_SIZE), F32)
    grid_spec = pltpu.PrefetchScalarGridSpec(
        num_scalar_prefetch=2, grid=(nb, t // bq),
        in_specs=[pl.BlockSpec((None, bq, w), lambda b, i, s, p: (b, i, 0)),
                  pl.BlockSpec((None, t, w), lambda b, i, s, p: (b, 0, 1)),
                  pl.BlockSpec((None, t, w), lambda b, i, s, p: (b, 0, 2)),
                  pl.BlockSpec((bq, bq), lambda b, i, s, p: (0, 0)),
                  pl.BlockSpec((nsamp, tq, 3 * w), sample_block),
                  pl.BlockSpec((PAGE_SIZE, PAGE_SIZE), lambda b, i, s, p: (0, 0)),
                  pl.BlockSpec(memory_space=pl.ANY), pl.BlockSpec(memory_space=pl.ANY)],
        out_specs=[pl.BlockSpec((None, bq, w), lambda b, i, s, p: (b, i, 0)),
                   pl.BlockSpec((nsamp, tq, w), sample_block)],
        scratch_shapes=[pltpu.VMEM((N_HEADS, t, LANES), BF16), pltpu.VMEM((N_HEADS, t, HEAD_DIM), BF16),
                        pltpu.VMEM((N_HEADS, bq, HEAD_DIM), F32), pltpu.VMEM((N_HEADS, bq, 1), F32),
                        page_ring, page_ring,
                        pltpu.VMEM((w, npg * PAGE_SIZE), BF16), pltpu.VMEM((w, npg * PAGE_SIZE), BF16),
                        pltpu.SemaphoreType.DMA((PAGE_RING_SLOTS, 2))])
    return pl.pallas_call(
        functools.partial(_attn_kernel, bq=bq, layer=layer, nact=nact), grid_spec=grid_spec,
        out_shape=[jax.ShapeDtypeStruct((nb, t, w), F32), jax.ShapeDtypeStruct((ns, tq, w), F32)],
        compiler_params=_cparams("arbitrary", "arbitrary"), name="sb_attention",
    )(sb_bias, page_table, qkv, qkv, qkv, tri, qkv_s, tri_page, cache_kt, cache_vt)


def _paged_sample(bias_ref, qkv, pages, tri, kt_sc, vt_sc):
    npg = len(pages)
    tq = qkv.shape[0]
    hd, w, ps = HEAD_DIM, W_GROUP, PAGE_SIZE
    nr = N_HEADS * tq
    for j, (k_ref, v_ref) in enumerate(pages):
        kt_sc[:, j * ps:(j + 1) * ps] = k_ref[...].reshape(w, ps).astype(BF16)
        vt_sc[:, j * ps:(j + 1) * ps] = v_ref[...].reshape(w, ps).astype(BF16)

    q, kn, vn = qkv[:, 0:w], qkv[:, w:2 * w], qkv[:, 2 * w:3 * w]
    lane_head = lax.broadcasted_iota(jnp.int32, (1, w), 1) // hd
    row = lax.broadcasted_iota(jnp.int32, (nr, 1), 0)
    row_head, row_t = row // tq, row % tq
    qbd = jnp.concatenate([jnp.where(lane_head == h, q, 0.0) for h in range(N_HEADS)], axis=0)
    bias = jnp.zeros((nr, 1), F32)
    for h in range(N_HEADS):
        bias = jnp.where(row_head == h, bias_ref[h], bias)

    run = jnp.zeros((nr, 1), F32)
    acc = jnp.zeros((nr, w), F32)
    for j in range(tq - 1, -1, -1):
        z = jnp.sum(qbd * kn[j:j + 1, :], axis=-1, keepdims=True) * SB_SCALE + bias
        seen = j < row_t
        run = run + jnp.where(seen, _softplus(z), 0.0)
        acc = acc + jnp.where(seen, jnp.exp(z - run), 0.0) * vn[j:j + 1, :]

    z_all = _dot(qbd.astype(BF16), kt_sc[...]) * SB_SCALE + bias
    z2 = jnp.concatenate([z_all[:, j * ps:(j + 1) * ps] for j in range(npg)], axis=0)
    in_page = _suffix_sum(_softplus(z2), tri)
    later = [None] * npg
    for j in range(npg - 1, -1, -1):
        later[j] = run
        run = run + in_page[j * nr:(j + 1) * nr, 0:1]
    p2 = jnp.exp(z2 - (in_page + jnp.concatenate(later, axis=0)))
    p_all = jnp.concatenate([p2[j * nr:(j + 1) * nr] for j in range(npg)], axis=1).astype(BF16)
    acc = acc + _dot_nt(p_all, vt_sc[...])
    out = jnp.where(lane_head == 0, acc[0:tq], 0.0)
    for h in range(1, N_HEADS):
        out = jnp.where(lane_head == h, acc[h * tq:(h + 1) * tq], out)
    return out


def _block_diag(wt):
    depth, nh, d, _ = wt.shape
    return jnp.einsum("lhij,hg->lhigj", wt, jnp.eye(nh, dtype=wt.dtype)).reshape(depth, nh * d, nh * d)


def _window_counts(p0, t):
    win = jnp.repeat(jnp.asarray(POOL_WINDOWS, jnp.int32), POOL_GROUP)
    pos = p0 + jnp.arange(t, dtype=jnp.int32)
    return jnp.minimum(win[None, :], pos[:, None] + 1).astype(F32).reshape(t, 1, W_GROUP)


def _to_time_major(a):
    return jnp.swapaxes(a, 0, 1)


def kernel(x_prompt, x_sample, cache_k, cache_v, state_rglru_h, state_conv_a, state_conv_b, state_pool, page_table,
           ln_in_g, ln_in_b, w_in, conv_a_w, conv_a_b, lru_wa, lru_ba, lru_wx, lru_bx, lru_lambda, conv_b_w,
           sb_bias, pool_w, pool_scale, w_out, ln1_g, ln1_b, w_ff1, w_ff2, ln2_g, ln2_b):
    depth = w_in.shape[0]
    alpha = (2.0 * depth) ** 0.25
    bp, tp, d = x_prompt.shape
    bs, ts, _ = x_sample.shape
    past_len = page_table.shape[1] * PAGE_SIZE
    w = W_GROUP

    tm_p = min(512, bp * tp)
    tm_s = min(512, bs * ts)
    bq = min(256, tp)
    tt_p = min(256, tp)
    cs_p = min(16, tt_p)

    cnt_p = _window_counts(0, tp)
    cnt_s = _window_counts(past_len, ts)
    tri_q = jnp.tril(jnp.ones((bq, bq), BF16), k=-1)
    tri_page = jnp.tril(jnp.ones((PAGE_SIZE, PAGE_SIZE), BF16))
    cache_kt = jnp.transpose(cache_k, (0, 1, 3, 4, 2))
    cache_vt = jnp.transpose(cache_v, (0, 1, 3, 4, 2))
    zeros = lambda *shape: jnp.zeros(shape, F32)

    nmix = (N_PROJ_GROUPS - 3) * w
    w_in_b = jnp.concatenate([w_in[..., :5 * w], w_in[..., 8 * w:], w_in[..., 5 * w:8 * w]], axis=-1).astype(BF16)
    stack_row = lambda a: a.reshape(depth, 1, -1)
    dense_w = (jnp.concatenate([w_out[:, :2 * w], w_out[:, 3 * w:]], axis=1).astype(BF16),
               w_out[:, 2 * w:3 * w].astype(BF16), stack_row(ln1_g), stack_row(ln1_b),
               w_ff1.astype(BF16), w_ff2.astype(BF16), stack_row(ln2_g), stack_row(ln2_b), w_in_b)

    xp, mix_p, qkv_p, k_stack, v_stack = _ln_inproj_call(x_prompt, ln_in_g, ln_in_b, w_in_b, nmix, bp, tp, tm_p, True)
    xs, mix_s, qkv_s = _ln_inproj_call(
        _to_time_major(x_sample).reshape(ts * bs, d), ln_in_g, ln_in_b, w_in_b, nmix, bs, ts, tm_s, False)

    mix_w = (conv_a_w, stack_row(conv_a_b), _block_diag(lru_wa).astype(BF16), stack_row(lru_ba),
             _block_diag(lru_wx).astype(BF16), stack_row(lru_bx), stack_row(lru_lambda), conv_b_w,
             _block_diag(pool_w).astype(BF16), stack_row(pool_scale))
    states_p = (zeros(1, bp, w), zeros(1, CONV_A - 1, bp, w), zeros(1, CONV_B - 1, bp, w), zeros(1, POOL_BUF, bp, w))
    states_s = (state_rglru_h, jnp.swapaxes(state_conv_a, 1, 2), jnp.swapaxes(state_conv_b, 1, 2),
                jnp.swapaxes(state_pool, 1, 2))

    outs = {name: [] for name in ("ks", "vs", "hp", "hs", "ap", "as", "bp", "bs", "pp", "ps")}
    for l in range(depth):
        yabd_p, h_n, a_n, b_n, p_n = _seqmix_call(mix_p, cnt_p, states_p, 0, mix_w, l, bp, tt_p, cs_p)
        outs["hp"].append(h_n)
        outs["ap"].append(a_n)
        outs["bp"].append(b_n)
        outs["pp"].append(p_n)
        yabd, h_n, a_n, b_n, p_n = _seqmix_call(mix_s, cnt_s, states_s, l, mix_w, l, bs, ts, 1)

        qkv = _to_time_major(qkv_s.reshape(ts, bs, 3 * w))
        yc_p, yc = _attn_call(qkv_p, qkv, cache_kt, cache_vt, page_table, sb_bias[l], tri_q, tri_page, l, bq)
        yc = _to_time_major(yc).reshape(ts * bs, w)

        res = _dense_call(yabd_p, yc_p, xp, dense_w, (k_stack, v_stack), l, nmix, bp, tp, tm_p, alpha, True)
        if l == depth - 1:
            y_prompt = res
        else:
            xp, mix_p, qkv_p, k_stack, v_stack = res
        res = _dense_call(yabd, yc, xs, dense_w, None, l, nmix, bs, ts, tm_s, alpha, False)
        if l == depth - 1:
            y_sample = _to_time_major(res.reshape(ts, bs, d))
        else:
            xs, mix_s, qkv_s = res
        outs["ks"].append(qkv[..., w:2 * w].reshape(bs, ts, N_HEADS, HEAD_DIM))
        outs["vs"].append(qkv[..., 2 * w:].reshape(bs, ts, N_HEADS, HEAD_DIM))
        outs["hs"].append(h_n)
        outs["as"].append(a_n)
        outs["bs"].append(b_n)
        outs["ps"].append(p_n)

    st = {name: jnp.stack(v) for name, v in outs.items()}
    heads = lambda a: a.reshape(depth, bp, tp, N_HEADS, HEAD_DIM)
    seq_first = lambda a: jnp.swapaxes(a, 1, 2)
    return (y_prompt, y_sample, heads(k_stack), heads(v_stack), st["ks"], st["vs"], st["hp"], st["hs"],
            seq_first(st["ap"]), seq_first(st["as"]), seq_first(st["bp"]), seq_first(st["bs"]),
            seq_first(st["pp"]), seq_first(st["ps"]))
```

```python
import functools

import jax
import jax.numpy as jnp
from jax import lax
from jax.experimental import pallas as pl
from jax.experimental.pallas import tpu as pltpu

F32 = jnp.float32
BF16 = jnp.bfloat16

N_HEADS = 4
W_GROUP = 256
HEAD_DIM = W_GROUP // N_HEADS
N_PROJ_GROUPS = 9
CONV_A = 4
CONV_B = 3
RGLRU_C = 8.0
POOL_WINDOWS = (2, 4, 8, 16)
POOL_BUF = max(POOL_WINDOWS) - 1
POOL_GROUP = W_GROUP // len(POOL_WINDOWS)
PAGE_SIZE = 128
LN_EPS = 1e-5
SB_SCALE = HEAD_DIM ** -0.5
LANES = 128
LOG2E = 1.4426950408889634

VMEM_LIMIT_BYTES = 56 * 1024 * 1024


def _cparams(*sem):
    return pltpu.CompilerParams(dimension_semantics=sem, vmem_limit_bytes=VMEM_LIMIT_BYTES)


def _layer_norm(x, g, b):
    mu = jnp.mean(x, axis=-1, keepdims=True)
    xc = x - mu
    var = jnp.mean(xc * xc, axis=-1, keepdims=True)
    return xc * lax.rsqrt(var + LN_EPS) * g + b


def _softplus(x):
    return jnp.maximum(x, 0.0) + jnp.log1p(jnp.exp(-jnp.abs(x)))


def _dot(a, b):
    return jnp.dot(a, b, preferred_element_type=F32)


def _dot_nt(a, b):
    return lax.dot_general(a, b, (((1,), (1,)), ((), ())), preferred_element_type=F32)


def _suffix_sum(x, tri):
    hi = x.astype(BF16)
    lo = (x - hi.astype(F32)).astype(BF16)
    return _dot(hi, tri) + _dot(lo, tri)


def _rows_from_blocks(src_ref, slab_ref, fn=lambda v: v):
    nb, tt, c = src_ref.shape
    for s in range(nb):
        v = fn(src_ref[s])
        for j in range(c // LANES):
            slab_ref[j, pl.ds(s, tt, stride=nb), :] = v[:, j * LANES:(j + 1) * LANES]
    return jnp.concatenate([slab_ref[j] for j in range(c // LANES)], axis=1)


def _blocks_from_rows(val, slab_ref, dst_ref, copies=()):
    nb, tt, c = dst_ref.shape
    for j in range(c // LANES):
        slab_ref[j] = val[:, j * LANES:(j + 1) * LANES]
    for s in range(nb):
        for j in range(c // LANES):
            piece = slab_ref[j, pl.ds(s, tt, stride=nb), :]
            dst_ref[s, :, j * LANES:(j + 1) * LANES] = piece
            for ref, c0 in copies:
                if c0 <= j * LANES < c0 + ref.shape[-1]:
                    ref[s, :, j * LANES - c0:(j + 1) * LANES - c0] = piece


def _emit_inproj(xb, w_ref, mix_ref, qkv_ref, kv_refs, slab_ref):
    ngroups, _, w = mix_ref.shape
    nmix = ngroups * w
    mix = _dot(xb, w_ref[:, :nmix])
    for grp in range(ngroups):
        mix_ref[grp] = mix[:, grp * w:(grp + 1) * w]
    if slab_ref is None:
        qkv_ref[...] = _dot(xb, w_ref[:, nmix:])
    else:
        k_ref, v_ref = kv_refs
        _blocks_from_rows(_dot(xb, w_ref[:, nmix:]), slab_ref, qkv_ref, ((k_ref, W_GROUP), (v_ref, 2 * W_GROUP)))


def _ln_inproj_kernel(x_ref, g_ref, b_ref, w_ref, *rest, batch_major):
    norm = lambda v: _layer_norm(v, g_ref[...], b_ref[...])
    if batch_major:
        xo_ref, mix_ref, qkv_ref, k_ref, v_ref, slab_x, slab_qkv = rest[2:]
        x0 = _rows_from_blocks(x_ref, slab_x, norm)
        xo_ref[...] = x0
        _emit_inproj(x0.astype(BF16), w_ref, mix_ref, qkv_ref, (k_ref, v_ref), slab_qkv)
    else:
        xo_ref, mix_ref, qkv_ref = rest
        x0 = norm(x_ref[...])
        xo_ref[...] = x0
        _emit_inproj(x0.astype(BF16), w_ref, mix_ref, qkv_ref, None, None)


def _dense_kernel(yabd_ref, yc_ref, x_ref, wabd_ref, wc_ref, g1_ref, b1_ref, w1_ref, w2_ref, g2_ref, b2_ref,
                  *rest, alpha, fc, batch_major, last):
    kv_refs = None
    if last:
        (y_ref, acc_ref), scratch = rest[:2], rest[2:]
    elif batch_major:
        win_ref, xo_ref, mix_ref, qkv_ref, k_ref, v_ref, acc_ref = rest[:1] + rest[3:9]
        kv_refs, scratch = (k_ref, v_ref), rest[9:]
    else:
        (win_ref, xo_ref, mix_ref, qkv_ref, acc_ref), scratch = rest[:5], rest[5:]
    yc = _rows_from_blocks(yc_ref, scratch[0]) if batch_major else yc_ref[...]
    mix = _dot(yabd_ref[...], wabd_ref[...]) + _dot(yc.astype(BF16), wc_ref[...])
    x1 = _layer_norm(alpha * x_ref[...] + mix, g1_ref[...], b1_ref[...])
    xb = x1.astype(BF16)
    for c in range(w1_ref.shape[1] // fc):
        h = jnp.maximum(_dot(xb, w1_ref[:, c * fc:(c + 1) * fc]), 0.0)
        part = _dot((h * h).astype(BF16), w2_ref[c * fc:(c + 1) * fc, :])
        if c == 0:
            acc_ref[...] = part
        else:
            acc_ref[...] += part
    x2 = _layer_norm(alpha * x1 + acc_ref[...], g2_ref[...], b2_ref[...])
    if not last:
        xo_ref[...] = x2
        _emit_inproj(x2.astype(BF16), win_ref, mix_ref, qkv_ref, kv_refs, scratch[1] if batch_major else None)
    elif batch_major:
        _blocks_from_rows(x2, scratch[1], y_ref)
    else:
        y_ref[...] = x2


def _resident(shape, index_map):
    return pl.BlockSpec(shape, index_map, pipeline_mode=pl.Buffered(1))


def _layer_block(a, layer):
    return _resident((None,) + a.shape[1:], lambda i: (layer, 0, 0))


def _slab(rows, c):
    return pltpu.VMEM((c // LANES, rows, LANES), F32)


def _row_specs(nb, t, tm, batch_major):
    rows = lambda c: (pl.BlockSpec((tm, c), lambda i: (i, 0)), (nb * t, c))
    if batch_major:
        blocks = lambda c: (pl.BlockSpec((nb, tm // nb, c), lambda i: (0, i, 0)), (nb, t, c))
    else:
        blocks = rows
    return rows, blocks


def _planes_spec(n, tm, c):
    return (pl.BlockSpec((c // W_GROUP, tm, W_GROUP), lambda i: (0, i, 0)), (c // W_GROUP, n, W_GROUP))


def _kv_stack_spec(depth, nb, t, tm, layer):
    return (pl.BlockSpec((None, nb, tm // nb, W_GROUP), lambda i: (layer, 0, i, 0)), (depth, nb, t, W_GROUP))


def _ln_inproj_call(x, g, b, w_in, nmix, nb, t, tm, batch_major):
    d = x.shape[-1]
    depth = w_in.shape[0]
    nqkv = w_in.shape[2] - nmix
    rows, blocks = _row_specs(nb, t, tm, batch_major)
    vec = _resident((1, d), lambda i: (0, 0))
    in_specs = [blocks(d)[0], vec, vec, _layer_block(w_in, 0)]
    operands = (x, g.reshape(1, d), b.reshape(1, d), w_in)
    out = [rows(d), _planes_spec(nb * t, tm, nmix), blocks(nqkv)]
    aliases = {}
    if batch_major:
        aliases = {len(operands): 3, len(operands) + 1: 4}
        in_specs += [pl.BlockSpec(memory_space=pl.ANY)] * 2
        operands += (jnp.zeros((depth, nb, t, W_GROUP), F32), jnp.zeros((depth, nb, t, W_GROUP), F32))
        out += [_kv_stack_spec(depth, nb, t, tm, 0)] * 2
    return pl.pallas_call(
        functools.partial(_ln_inproj_kernel, batch_major=batch_major), grid=(nb * t // tm,),
        in_specs=in_specs, out_specs=[s for s, _ in out],
        out_shape=[jax.ShapeDtypeStruct(shp, F32) for _, shp in out],
        scratch_shapes=[_slab(tm, d), _slab(tm, nqkv)] if batch_major else [],
        input_output_aliases=aliases, compiler_params=_cparams("arbitrary"), name="ln_in_proj")(*operands)


def _dense_call(yabd, yc, x, wts, kv_stacks, layer, nmix, nb, t, tm, alpha, batch_major, fc=512):
    d = x.shape[-1]
    w_in = wts[-1]
    depth = w_in.shape[0]
    last = layer == depth - 1
    nqkv = w_in.shape[2] - nmix
    rows, blocks = _row_specs(nb, t, tm, batch_major)
    in_specs = [rows(yabd.shape[-1])[0], blocks(yc.shape[-1])[0], rows(d)[0]] + [_layer_block(a, layer) for a in wts[:-1]]
    scratch = [pltpu.VMEM((tm, d), F32)]
    aliases = {}
    if batch_major:
        scratch += [_slab(tm, yc.shape[-1]), _slab(tm, d if last else nqkv)]
    if last:
        out = [blocks(d)]
        operands = (yabd, yc, x) + tuple(wts[:-1])
    else:
        in_specs.append(_layer_block(w_in, layer + 1))
        out = [rows(d), _planes_spec(nb * t, tm, nmix), blocks(nqkv)]
        operands = (yabd, yc, x) + tuple(wts)
        if batch_major:
            aliases = {len(operands): 3, len(operands) + 1: 4}
            in_specs += [pl.BlockSpec(memory_space=pl.ANY)] * 2
            out += [_kv_stack_spec(depth, nb, t, tm, layer + 1)] * 2
            operands += tuple(kv_stacks)
    res = pl.pallas_call(
        functools.partial(_dense_kernel, alpha=alpha, fc=fc, batch_major=batch_major, last=last),
        grid=(nb * t // tm,), in_specs=in_specs,
        out_specs=[s for s, _ in out], out_shape=[jax.ShapeDtypeStruct(shp, F32) for _, shp in out],
        scratch_shapes=scratch, input_output_aliases=aliases,
        compiler_params=_cparams("arbitrary"), name="dense_layer")(*operands)
    return res[0] if last else res


def _seqmix_kernel(xa_ref, ga_ref, xb_ref, gb_ref, gc_ref, xp_ref, cnt_ref,
                   h0_ref, bufa_ref, bufb_ref, bufp_ref,
                   caw_ref, cab_ref, wa_ref, ba_ref, wx_ref, bx_ref, lam_ref, cbw_ref, wp_ref, ps_ref,
                   y_ref, ho_ref, ao_ref, bo_ref, po_ref,
                   ea, eb, ep, h_sc, *, tt, nb, cs):
    w = W_GROUP
    i = pl.program_id(0)

    @pl.when(i == 0)
    def _():
        ea[0:CONV_A - 1] = bufa_ref[...]
        eb[0:CONV_B - 1] = bufb_ref[...]
        ep[0:POOL_BUF] = bufp_ref[...]
        h_sc[...] = h0_ref[...]

    @pl.when(i > 0)
    def _():
        ea[0:CONV_A - 1] = ea[tt:tt + CONV_A - 1]
        eb[0:CONV_B - 1] = eb[tt:tt + CONV_B - 1]
        ep[0:POOL_BUF] = ep[tt:tt + POOL_BUF]

    ea[CONV_A - 1:CONV_A - 1 + tt] = xa_ref[...].reshape(tt, nb, w)
    ep[POOL_BUF:POOL_BUF + tt] = xp_ref[...].reshape(tt, nb, w)

    coef = -RGLRU_C * _softplus(-lam_ref[...])
    group = lax.broadcasted_iota(jnp.int32, (1, 1, w), 2) // POOL_GROUP
    rows = cs * nb

    def chunk(k, carry):
        t0 = pl.multiple_of(k * cs, cs)
        r0 = pl.multiple_of(k * rows, rows)

        xc = caw_ref[0:1, :][None] * ea[pl.ds(t0, cs)]
        for j in range(1, CONV_A):
            xc = xc + caw_ref[j:j + 1, :][None] * ea[pl.ds(t0 + j, cs)]
        xc = (xc + cab_ref[...][None]).reshape(rows, w)
        xcb = xc.astype(BF16)
        r = jax.nn.sigmoid(_dot(xcb, wa_ref[...]) + ba_ref[...])
        gi = jax.nn.sigmoid(_dot(xcb, wx_ref[...]) + bx_ref[...])
        log_a = coef * r
        a = jnp.exp(log_a)
        a3 = a.reshape(cs, nb, w)
        b3 = (jnp.sqrt(jnp.tanh(-log_a) * (a * a + 1.0)) * (gi * xc)).reshape(cs, nb, w)
        h = h_sc[...]
        hs = []
        for t in range(cs):
            h = a3[t] * h + b3[t]
            hs.append(h[None])
        h_sc[...] = h
        hseq = jnp.concatenate(hs, axis=0).reshape(rows, w)
        y_ref[pl.ds(r0, rows), 0:w] = (hseq * jax.nn.gelu(ga_ref[pl.ds(r0, rows), :])).astype(y_ref.dtype)

        u = gc_ref[pl.ds(r0, rows), :] * xb_ref[pl.ds(r0, rows), :]
        eb[pl.ds(t0 + CONV_B - 1, cs)] = u.reshape(cs, nb, w)
        cb = cbw_ref[0:1, :][None] * eb[pl.ds(t0, cs)]
        for j in range(1, CONV_B):
            cb = cb + cbw_ref[j:j + 1, :][None] * eb[pl.ds(t0 + j, cs)]
        y_ref[pl.ds(r0, rows), w:2 * w] = (gb_ref[pl.ds(r0, rows), :] * cb.reshape(rows, w)).astype(y_ref.dtype)

        e = ep[pl.ds(t0, cs + POOL_BUF)]
        s2 = e[1:] + e[:-1]
        s4 = s2[2:] + s2[:-2]
        s8 = s4[4:] + s4[:-4]
        s16 = s8[8:] + s8[:-8]
        win = jnp.where(group == 0, s2[14:], jnp.where(group == 1, s4[12:], jnp.where(group == 2, s8[8:], s16)))
        d = (win / cnt_ref[pl.ds(t0, cs)] - e[POOL_BUF:]).reshape(rows, w)
        y_ref[pl.ds(r0, rows), 2 * w:3 * w] = (_dot(d.astype(BF16), wp_ref[...]) * ps_ref[...]).astype(y_ref.dtype)
        return carry

    lax.fori_loop(0, tt // cs, chunk, 0)

    ho_ref[...] = h_sc[...]
    ao_ref[...] = ea[tt:tt + CONV_A - 1]
    bo_ref[...] = eb[tt:tt + CONV_B - 1]
    po_ref[...] = ep[tt:tt + POOL_BUF]


def _seqmix_call(mix, cnt, states, state_layer, wts, layer, nb, tt, cs):
    n = mix.shape[1]
    t_total = n // nb
    w = W_GROUP
    assert t_total % tt == 0 and tt % cs == 0 and (tt >= POOL_BUF or tt == t_total)
    plane = lambda c: pl.BlockSpec((None, tt * nb, w), lambda i, c=c: (c, i, 0))
    entry = lambda a, j: pl.BlockSpec((None,) + a.shape[1:], lambda i: (j,) + (0,) * (a.ndim - 1))
    whole = lambda a: pl.BlockSpec(a.shape, lambda i: (0,) * len(a.shape))
    state_shapes = [jax.ShapeDtypeStruct(a.shape[1:], F32) for a in states]
    return pl.pallas_call(
        functools.partial(_seqmix_kernel, tt=tt, nb=nb, cs=cs), grid=(t_total // tt,),
        in_specs=[plane(c) for c in range(mix.shape[0])] + [pl.BlockSpec((tt, 1, w), lambda i: (i, 0, 0))]
        + [entry(a, state_layer) for a in states] + [entry(a, layer) for a in wts],
        out_specs=[pl.BlockSpec((tt * nb, 3 * w), lambda i: (i, 0))] + [whole(a) for a in state_shapes],
        out_shape=[jax.ShapeDtypeStruct((n, 3 * w), BF16)] + state_shapes,
        scratch_shapes=[pltpu.VMEM((tt + CONV_A - 1, nb, w), F32), pltpu.VMEM((tt + CONV_B - 1, nb, w), F32),
                        pltpu.VMEM((tt + POOL_BUF, nb, w), F32), pltpu.VMEM((nb, w), F32)],
        compiler_params=_cparams("arbitrary"), name="seq_mixers",
    )(*([mix] * mix.shape[0]), cnt, *states, *wts)


def _attn_kernel(bias_ref, pt_ref, q_ref, k_ref, v_ref, tri_ref, qkvs_ref, trip_ref, ckt_ref, cvt_ref,
                 o_ref, os_ref, kb, vb, acc_sc, c_sc, kpage, vpage, kt_sc, vt_sc, sem, *, bq, layer, nact):
    i = pl.program_id(1)
    hd = HEAD_DIM
    t = k_ref.shape[0]
    heads = range(N_HEADS)
    pad = kb.shape[2] - hd - 2

    step = pl.program_id(0) * pl.num_programs(1) + i
    nsamp, tq, _ = qkvs_ref.shape
    npg = kpage.shape[1] // nsamp

    def page_copies(st, slot):
        copies = []
        for k in range(nsamp):
            for j in range(npg):
                page = pt_ref[st * nsamp + k, j]
                copies.append(pltpu.make_async_copy(ckt_ref.at[layer, page], kpage.at[slot, k * npg + j], sem.at[slot, 0]))
                copies.append(pltpu.make_async_copy(cvt_ref.at[layer, page], vpage.at[slot, k * npg + j], sem.at[slot, 1]))
        return copies

    @pl.when(step == 0)
    def _():
        for n, cp in enumerate(page_copies(0, 0)):
            cp.start(priority=n // 2 % 2)

    @pl.when(step + 1 < nact)
    def _():
        for n, cp in enumerate(page_copies(step + 1, (step + 1) % 2)):
            cp.start(priority=n // 2 % 2)

    @pl.when(step < nact)
    def _():
        slot = step % 2
        for cp in page_copies(step, slot):
            cp.wait()
        for k in range(nsamp):
            pages = [(kpage.at[slot, k * npg + j], vpage.at[slot, k * npg + j]) for j in range(npg)]
            os_ref[k] = _paged_sample(bias_ref, qkvs_ref[k], pages, trip_ref[...], kt_sc, vt_sc)

    @pl.when(i == 0)
    def _():
        for h in heads:
            b = jnp.full((t, 1), bias_ref[h], F32)
            b_hi = b.astype(BF16)
            b_lo = (b - b_hi.astype(F32)).astype(BF16)
            kb[h] = jnp.concatenate(
                [k_ref[:, h * hd:(h + 1) * hd].astype(BF16), b_hi, b_lo, jnp.zeros((t, pad), BF16)], axis=1)
            vb[h] = v_ref[:, h * hd:(h + 1) * hd].astype(BF16)

    tri = tri_ref[...]
    q_tail = jnp.concatenate([jnp.ones((bq, 2), BF16), jnp.zeros((bq, pad), BF16)], axis=1)
    qs = [jnp.concatenate([(q_ref[:, h * hd:(h + 1) * hd] * SB_SCALE).astype(BF16), q_tail], axis=1) for h in heads]
    causal = lax.broadcasted_iota(jnp.int32, (bq, bq), 1) < lax.broadcasted_iota(jnp.int32, (bq, bq), 0)

    def sweep(blocks, first):
        rows = [pl.ds(pl.multiple_of(j * bq, bq), bq) for j, _ in blocks]
        units = [(n, h) for n in range(len(blocks)) for h in heads]
        carry = [None if first else c_sc[h] for h in heads]
        pvs = [None] * N_HEADS
        z, ls, col0, later = {}, {}, {}, {}
        for k in range(len(units) + 2):
            if k < len(units):
                n, h = units[k]
                z[k] = _dot_nt(qs[h], kb[h, rows[n], :])
            if 1 <= k <= len(units):
                mask = blocks[units[k - 1][0]][1]
                zz = z.pop(k - 1)
                sp = jnp.maximum(zz, 0.0) + jnp.log(1.0 + jnp.exp2(jnp.abs(zz) * -LOG2E))
                ls[k - 1] = zz - sp
                if mask is not None:
                    sp = jnp.where(mask, sp, 0.0)
                col0[k - 1] = sp[:, 0:1]
                later[k - 1] = _dot(sp.astype(BF16), tri)
            if 2 <= k:
                n, h = units[k - 2]
                mask = blocks[n][1]
                lt = later.pop(k - 2)
                e = ls.pop(k - 2) - lt
                if carry[h] is not None:
                    e = e - carry[h]
                p = jnp.exp(e)
                if mask is not None:
                    p = jnp.where(mask, p, 0.0)
                total = lt[:, 0:1] + col0.pop(k - 2)
                carry[h] = total if carry[h] is None else carry[h] + total
                pv = _dot(p.astype(BF16), vb[h, rows[n], :])
                pvs[h] = pv if pvs[h] is None else pvs[h] + pv
        for h in heads:
            c_sc[h] = carry[h]
            acc_sc[h] = pvs[h] if first else acc_sc[h] + pvs[h]

    @pl.when(i % 2 == 0)
    def _():
        sweep([(i, causal)], True)

    @pl.when(i % 2 == 1)
    def _():
        sweep([(i, causal), (i - 1, None)], True)

    top = i - 1 - i % 2

    def body(pair, carry):
        j = top - 2 * pair
        sweep([(j, None), (j - 1, None)], False)
        return carry

    lax.fori_loop(0, i // 2, body, 0)
    o_ref[...] = jnp.concatenate([acc_sc[h] for h in heads], axis=-1)


def _attn_call(qkv, qkv_s, cache_kt, cache_vt, page_table, sb_bias, tri, tri_page, layer, bq):
    nb, t, _ = qkv.shape
    ns, tq, _ = qkv_s.shape
    npg = page_table.shape[1]
    w = W_GROUP
    nsteps = nb * (t // bq)
    nsamp = -(-ns // nsteps)
    nact = ns // nsamp
    assert ns % nsamp == 0 and (N_HEADS * tq) % 8 == 0
    sample_block = lambda b, i, s, p: (jnp.minimum(b * (t // bq) + i, nact - 1), 0, 0)
    page_ring = pltpu.VMEM((2, nsamp * npg, N_HEADS, HEAD_DIM, PAGE_SIZE), F32)
    grid_spec = pltpu.PrefetchScalarGridSpec(
        num_scalar_prefetch=2, grid=(nb, t // bq),
        in_specs=[pl.BlockSpec((None, bq, w), lambda b, i, s, p: (b, i, 0)),
                  pl.BlockSpec((None, t, w), lambda b, i, s, p: (b, 0, 1)),
                  pl.BlockSpec((None, t, w), lambda b, i, s, p: (b, 0, 2)),
                  pl.BlockSpec((bq, bq), lambda b, i, s, p: (0, 0)),
                  pl.BlockSpec((nsamp, tq, 3 * w), sample_block),
                  pl.BlockSpec((PAGE_SIZE, PAGE_SIZE), lambda b, i, s, p: (0, 0)),
                  pl.BlockSpec(memory_space=pl.ANY), pl.BlockSpec(memory_space=pl.ANY)],
        out_specs=[pl.BlockSpec((None, bq, w), lambda b, i, s, p: (b, i, 0)),
                   pl.BlockSpec((nsamp, tq, w), sample_block)],
        scratch_shapes=[pltpu.VMEM((N_HEADS, t, LANES), BF16), pltpu.VMEM((N_HEADS, t, HEAD_DIM), BF16),
                        pltpu.VMEM((N_HEADS, bq, HEAD_DIM), F32), pltpu.VMEM((N_HEADS, bq, 1), F32),
                        page_ring, page_ring,
                        pltpu.VMEM((w, npg * PAGE_SIZE), BF16), pltpu.VMEM((w, npg * PAGE_SIZE), BF16),
                        pltpu.SemaphoreType.DMA((2, 2))])
    return pl.pallas_call(
        functools.partial(_attn_kernel, bq=bq, layer=layer, nact=nact), grid_spec=grid_spec,
        out_shape=[jax.ShapeDtypeStruct((nb, t, w), F32), jax.ShapeDtypeStruct((ns, tq, w), F32)],
        compiler_params=_cparams("arbitrary", "arbitrary"), name="sb_attention",
    )(sb_bias, page_table, qkv, qkv, qkv, tri, qkv_s, tri_page, cache_kt, cache_vt)


def _paged_sample(bias_ref, qkv, pages, tri, kt_sc, vt_sc):
    npg = len(pages)
    tq = qkv.shape[0]
    hd, w, ps = HEAD_DIM, W_GROUP, PAGE_SIZE
    nr = N_HEADS * tq
    for j, (k_ref, v_ref) in enumerate(pages):
        kt_sc[:, j * ps:(j + 1) * ps] = k_ref[...].reshape(w, ps).astype(BF16)
        vt_sc[:, j * ps:(j + 1) * ps] = v_ref[...].reshape(w, ps).astype(BF16)

    q, kn, vn = qkv[:, 0:w], qkv[:, w:2 * w], qkv[:, 2 * w:3 * w]
    lane_head = lax.broadcasted_iota(jnp.int32, (1, w), 1) // hd
    row = lax.broadcasted_iota(jnp.int32, (nr, 1), 0)
    row_head, row_t = row // tq, row % tq
    qbd = jnp.concatenate([jnp.where(lane_head == h, q, 0.0) for h in range(N_HEADS)], axis=0)
    bias = jnp.zeros((nr, 1), F32)
    for h in range(N_HEADS):
        bias = jnp.where(row_head == h, bias_ref[h], bias)

    run = jnp.zeros((nr, 1), F32)
    acc = jnp.zeros((nr, w), F32)
    for j in range(tq - 1, -1, -1):
        z = jnp.sum(qbd * kn[j:j + 1, :], axis=-1, keepdims=True) * SB_SCALE + bias
        seen = j < row_t
        run = run + jnp.where(seen, _softplus(z), 0.0)
        acc = acc + jnp.where(seen, jnp.exp(z - run), 0.0) * vn[j:j + 1, :]

    z_all = _dot(qbd.astype(BF16), kt_sc[...]) * SB_SCALE + bias
    z2 = jnp.concatenate([z_all[:, j * ps:(j + 1) * ps] for j in range(npg)], axis=0)
    in_page = _suffix_sum(_softplus(z2), tri)
    later = [None] * npg
    for j in range(npg - 1, -1, -1):
        later[j] = run
        run = run + in_page[j * nr:(j + 1) * nr, 0:1]
    p2 = jnp.exp(z2 - (in_page + jnp.concatenate(later, axis=0)))
    p_all = jnp.concatenate([p2[j * nr:(j + 1) * nr] for j in range(npg)], axis=1).astype(BF16)
    acc = acc + _dot_nt(p_all, vt_sc[...])
    out = jnp.where(lane_head == 0, acc[0:tq], 0.0)
    for h in range(1, N_HEADS):
        out = jnp.where(lane_head == h, acc[h * tq:(h + 1) * tq], out)
    return out


def _block_diag(wt):
    depth, nh, d, _ = wt.shape
    return jnp.einsum("lhij,hg->lhigj", wt, jnp.eye(nh, dtype=wt.dtype)).reshape(depth, nh * d, nh * d)


def _window_counts(p0, t):
    win = jnp.repeat(jnp.asarray(POOL_WINDOWS, jnp.int32), POOL_GROUP)
    pos = p0 + jnp.arange(t, dtype=jnp.int32)
    return jnp.minimum(win[None, :], pos[:, None] + 1).astype(F32).reshape(t, 1, W_GROUP)


def _to_time_major(a):
    return jnp.swapaxes(a, 0, 1)


def kernel(x_prompt, x_sample, cache_k, cache_v, state_rglru_h, state_conv_a, state_conv_b, state_pool, page_table,
           ln_in_g, ln_in_b, w_in, conv_a_w, conv_a_b, lru_wa, lru_ba, lru_wx, lru_bx, lru_lambda, conv_b_w,
           sb_bias, pool_w, pool_scale, w_out, ln1_g, ln1_b, w_ff1, w_ff2, ln2_g, ln2_b):
    depth = w_in.shape[0]
    alpha = (2.0 * depth) ** 0.25
    bp, tp, d = x_prompt.shape
    bs, ts, _ = x_sample.shape
    past_len = page_table.shape[1] * PAGE_SIZE
    w = W_GROUP

    tm_p = min(512, bp * tp)
    tm_s = min(512, bs * ts)
    bq = min(256, tp)
    tt_p = min(256, tp)
    cs_p = min(16, tt_p)

    cnt_p = _window_counts(0, tp)
    cnt_s = _window_counts(past_len, ts)
    tri_q = jnp.tril(jnp.ones((bq, bq), BF16), k=-1)
    tri_page = jnp.tril(jnp.ones((PAGE_SIZE, PAGE_SIZE), BF16))
    cache_kt = jnp.transpose(cache_k, (0, 1, 3, 4, 2))
    cache_vt = jnp.transpose(cache_v, (0, 1, 3, 4, 2))
    zeros = lambda *shape: jnp.zeros(shape, F32)

    nmix = (N_PROJ_GROUPS - 3) * w
    w_in_b = jnp.concatenate([w_in[..., :5 * w], w_in[..., 8 * w:], w_in[..., 5 * w:8 * w]], axis=-1).astype(BF16)
    stack_row = lambda a: a.reshape(depth, 1, -1)
    dense_w = (jnp.concatenate([w_out[:, :2 * w], w_out[:, 3 * w:]], axis=1).astype(BF16),
               w_out[:, 2 * w:3 * w].astype(BF16), stack_row(ln1_g), stack_row(ln1_b),
               w_ff1.astype(BF16), w_ff2.astype(BF16), stack_row(ln2_g), stack_row(ln2_b), w_in_b)

    xp, mix_p, qkv_p, k_stack, v_stack = _ln_inproj_call(x_prompt, ln_in_g, ln_in_b, w_in_b, nmix, bp, tp, tm_p, True)
    xs, mix_s, qkv_s = _ln_inproj_call(
        _to_time_major(x_sample).reshape(ts * bs, d), ln_in_g, ln_in_b, w_in_b, nmix, bs, ts, tm_s, False)

    mix_w = (conv_a_w, stack_row(conv_a_b), _block_diag(lru_wa).astype(BF16), stack_row(lru_ba),
             _block_diag(lru_wx).astype(BF16), stack_row(lru_bx), stack_row(lru_lambda), conv_b_w,
             _block_diag(pool_w).astype(BF16), stack_row(pool_scale))
    states_p = (zeros(1, bp, w), zeros(1, CONV_A - 1, bp, w), zeros(1, CONV_B - 1, bp, w), zeros(1, POOL_BUF, bp, w))
    states_s = (state_rglru_h, jnp.swapaxes(state_conv_a, 1, 2), jnp.swapaxes(state_conv_b, 1, 2),
                jnp.swapaxes(state_pool, 1, 2))

    outs = {name: [] for name in ("ks", "vs", "hp", "hs", "ap", "as", "bp", "bs", "pp", "ps")}
    for l in range(depth):
        yabd_p, h_n, a_n, b_n, p_n = _seqmix_call(mix_p, cnt_p, states_p, 0, mix_w, l, bp, tt_p, cs_p)
        outs["hp"].append(h_n)
        outs["ap"].append(a_n)
        outs["bp"].append(b_n)
        outs["pp"].append(p_n)
        yabd, h_n, a_n, b_n, p_n = _seqmix_call(mix_s, cnt_s, states_s, l, mix_w, l, bs, ts, 1)

        qkv = _to_time_major(qkv_s.reshape(ts, bs, 3 * w))
        yc_p, yc = _attn_call(qkv_p, qkv, cache_kt, cache_vt, page_table, sb_bias[l], tri_q, tri_page, l, bq)
        yc = _to_time_major(yc).reshape(ts * bs, w)

        res = _dense_call(yabd_p, yc_p, xp, dense_w, (k_stack, v_stack), l, nmix, bp, tp, tm_p, alpha, True)
        if l == depth - 1:
            y_prompt = res
        else:
            xp, mix_p, qkv_p, k_stack, v_stack = res
        res = _dense_call(yabd, yc, xs, dense_w, None, l, nmix, bs, ts, tm_s, alpha, False)
        if l == depth - 1:
            y_sample = _to_time_major(res.reshape(ts, bs, d))
        else:
            xs, mix_s, qkv_s = res
        outs["ks"].append(qkv[..., w:2 * w].reshape(bs, ts, N_HEADS, HEAD_DIM))
        outs["vs"].append(qkv[..., 2 * w:].reshape(bs, ts, N_HEADS, HEAD_DIM))
        outs["hs"].append(h_n)
        outs["as"].append(a_n)
        outs["bs"].append(b_n)
        outs["ps"].append(p_n)

    st = {name: jnp.stack(v) for name, v in outs.items()}
    heads = lambda a: a.reshape(depth, bp, tp, N_HEADS, HEAD_DIM)
    seq_first = lambda a: jnp.swapaxes(a, 1, 2)
    return (y_prompt, y_sample, heads(k_stack), heads(v_stack), st["ks"], st["vs"], st["hp"], st["hs"],
            seq_first(st["ap"]), seq_first(st["as"]), seq_first(st["bp"]), seq_first(st["bs"]),
            seq_first(st["pp"]), seq_first(st["ps"]))
```
